```python
import jax, jax.numpy as jnp
from jax import lax
import numpy as np

D_MODEL = 1024
BATCH = 8
SEQ = 2048
DEPTH = 1

CTX_LEN = 256
GRID_W = 64
CHUNK = 64
N_DIR = 2
GLA_HEADS = 4
GLA_DK = 64
GLA_DV = 128
GLA_RANK = 16
GLA_TAU = 16.0
GLA_QK = GLA_HEADS * GLA_DK
GLA_V = GLA_HEADS * GLA_DV
GDN_HEADS = 4
GDN_DK = 128
GDN_DV = 128
GDN_QK = GDN_HEADS * GDN_DK
GDN_V = GDN_HEADS * GDN_DV
CONV_K = 3
CONV_CH = 2 * GDN_QK + GDN_V
EPS = 1e-6
SPLIT_SIZES = (GLA_QK, GLA_QK, GLA_V, GLA_V, N_DIR * GLA_RANK,
               GDN_QK, GDN_QK, GDN_V, GDN_V, N_DIR * GDN_HEADS, N_DIR * GDN_HEADS,
               2 * D_MODEL)
D_IN = sum(SPLIT_SIZES)

kernel_name = 'hybrid_gla_gdn_prefix_dit_block'


def rmsnorm(x, g):
    xf = x.astype(jnp.float32)
    y = xf * lax.rsqrt(jnp.mean(xf * xf, axis=-1, keepdims=True) + EPS)
    return (y * g.astype(jnp.float32)).astype(x.dtype)


def l2norm(x):
    return x * lax.rsqrt(jnp.sum(x * x, axis=-1, keepdims=True) + EPS)


def rev(a):
    return jnp.flip(a, axis=1)


def split_cols(z):
    outs = []
    start = 0
    for size in SPLIT_SIZES:
        outs.append(z[..., start:start + size])
        start += size
    return outs


def conv_grid(u, w):
    b, t, ch = u.shape
    rows = t // GRID_W
    ug = u.reshape(b, rows, GRID_W, ch)
    out = lax.conv_general_dilated(ug, w[:, :, None, :].astype(u.dtype), window_strides=(1, 1), padding='SAME',
                                   dimension_numbers=('NHWC', 'HWIO', 'NHWC'), feature_group_count=ch)
    return out.reshape(b, t, ch)


def conv_seq(u, w):
    ch = u.shape[-1]
    return lax.conv_general_dilated(u, w[1][:, None, :].astype(u.dtype), window_strides=(1,), padding='SAME',
                                    dimension_numbers=('NWC', 'WIO', 'NWC'), feature_group_count=ch)


def to_chunks(a):
    b, t, h, d = a.shape
    return a.reshape(b, t // CHUNK, CHUNK, h, d).transpose(1, 0, 3, 2, 4)


def from_chunks(a):
    n, b, h, c, d = a.shape
    return a.transpose(1, 0, 3, 2, 4).reshape(b, n * c, h, d)


def gla_chunked(q, k, v, g, s0):
    qc, kc, vc, gc = to_chunks(q), to_chunks(k), to_chunks(v), to_chunks(g)
    bc = jnp.cumsum(gc, axis=3)
    causal = jnp.tril(jnp.ones((CHUNK, CHUNK), dtype=bool))

    def step(s, inp):
        qi, ki, vi, bi = inp
        diff = bi[:, :, :, None, :] - bi[:, :, None, :, :]
        dec = jnp.exp(jnp.where(causal[:, :, None], diff, -jnp.inf))
        att = jnp.einsum('bhtd,bhsd,bhtsd->bhts', qi, ki, dec)
        o = jnp.einsum('bhtd,bhde->bhte', qi * jnp.exp(bi), s) + jnp.einsum('bhts,bhse->bhte', att, vi)
        bl = bi[:, :, -1:, :]
        s_new = s * jnp.exp(bl[:, :, 0, :])[..., None] + jnp.einsum('bhsd,bhse->bhde', ki * jnp.exp(bl - bi), vi)
        return s_new, o

    s_fin, o = lax.scan(step, s0, (qc, kc, vc, bc))
    return from_chunks(o), s_fin


def gdn_chunked(q, k, v, g, beta, s0):
    qc, kc, vc = to_chunks(q), to_chunks(k), to_chunks(v)
    gc = to_chunks(g[..., None])[..., 0]
    bc = to_chunks(beta[..., None])
    gam = jnp.cumsum(gc, axis=-1)
    incl = jnp.tril(jnp.ones((CHUNK, CHUNK), dtype=bool))
    strict = jnp.tril(jnp.ones((CHUNK, CHUNK), dtype=bool), -1)
    dmask = jnp.exp(jnp.where(incl, gam[..., :, None] - gam[..., None, :], -jnp.inf))
    kb = kc * bc
    vb = vc * bc
    low = jnp.where(strict, jnp.einsum('nbhid,nbhjd->nbhij', kb, kc) * dmask, 0.0)
    eye = jnp.eye(CHUNK, dtype=low.dtype)
    tmat = lax.linalg.triangular_solve(eye + low, jnp.broadcast_to(eye, low.shape), left_side=True, lower=True)
    u = jnp.matmul(tmat, vb)
    w = jnp.matmul(tmat, kb * jnp.exp(gam)[..., None])
    aqk = jnp.einsum('nbhid,nbhjd->nbhij', qc, kc) * dmask

    def step(s, inp):
        qi, ki, ui, wi, ai, gi = inp
        v_new = ui - jnp.matmul(wi, s)
        o = jnp.matmul(qi * jnp.exp(gi)[..., None], s) + jnp.matmul(ai, v_new)
        gl = gi[..., -1]
        s_new = s * jnp.exp(gl)[..., None, None] + jnp.einsum('bhcd,bhce->bhde', ki * jnp.exp(gl[..., None] - gi)[..., None], v_new)
        return s_new, o

    s_fin, o = lax.scan(step, s0, (qc, kc, u, w, aqk, gam))
    return from_chunks(o), s_fin


def branch_inputs(h, w_in, gla_up, gla_ub, gdn_conv, gdn_a_log, gdn_dt_bias, on_grid):
    f32 = jnp.float32
    b, t, _ = h.shape
    z = jnp.matmul(h, w_in).astype(f32)
    aq, ak, av, ag, alr, bq, bk, bv, bg, bbeta, bdec, mg = split_cols(z)
    aq = aq.reshape(b, t, GLA_HEADS, GLA_DK) * (GLA_DK ** -0.5)
    ak = ak.reshape(b, t, GLA_HEADS, GLA_DK)
    av = av.reshape(b, t, GLA_HEADS, GLA_DV)
    alr = alr.reshape(b, t, N_DIR, GLA_RANK)
    alog = jax.nn.log_sigmoid(jnp.einsum('btnr,nrk->btnk', alr, gla_up.astype(f32)) + gla_ub.astype(f32)) / GLA_TAU
    alog = alog.reshape(b, t, N_DIR, GLA_HEADS, GLA_DK)
    qkv = jnp.concatenate([bq, bk, bv], axis=-1)
    qkv = jax.nn.silu(conv_grid(qkv, gdn_conv) if on_grid else conv_seq(qkv, gdn_conv))
    bq = l2norm(qkv[..., :GDN_QK].reshape(b, t, GDN_HEADS, GDN_DK)) * (GDN_DK ** -0.5)
    bk = l2norm(qkv[..., GDN_QK:2 * GDN_QK].reshape(b, t, GDN_HEADS, GDN_DK))
    bv = qkv[..., 2 * GDN_QK:].reshape(b, t, GDN_HEADS, GDN_DV)
    beta = jax.nn.sigmoid(bbeta.reshape(b, t, N_DIR, GDN_HEADS))
    glog = -jnp.exp(gdn_a_log.astype(f32)) * jax.nn.softplus(bdec.reshape(b, t, N_DIR, GDN_HEADS) + gdn_dt_bias.astype(f32))
    return aq, ak, av, ag, alog, bq, bk, bv, bg, beta, glog, mg


def run_mixers(aq, ak, av, alog, bq, bk, bv, beta, glog, sa_f, sa_b, sb_f, sb_b):
    oa_f, sa_f = gla_chunked(aq, ak, av, alog[:, :, 0], sa_f)
    oa_b, sa_b = gla_chunked(rev(aq), rev(ak), rev(av), rev(alog[:, :, 1]), sa_b)
    ob_f, sb_f = gdn_chunked(bq, bk, bv, glog[:, :, 0], beta[:, :, 0], sb_f)
    ob_b, sb_b = gdn_chunked(rev(bq), rev(bk), rev(bv), rev(glog[:, :, 1]), rev(beta[:, :, 1]), sb_b)
    return oa_f + rev(oa_b), ob_f + rev(ob_b), sa_f, sa_b, sb_f, sb_b


def branch_merge(oa, ob, ag, bg, mg, gla_onorm, gdn_onorm, w_gla_out, w_gdn_out, b_gate, w_o):
    b, t = oa.shape[0], oa.shape[1]
    ya = rmsnorm(oa, gla_onorm).reshape(b, t, GLA_V) * jax.nn.silu(ag)
    yb = rmsnorm(ob, gdn_onorm).reshape(b, t, GDN_V) * jax.nn.silu(bg)
    gates = jax.nn.sigmoid(mg + b_gate.astype(jnp.float32))
    merged = gates[..., :D_MODEL] * jnp.matmul(ya, w_gla_out) + gates[..., D_MODEL:] * jnp.matmul(yb, w_gdn_out)
    return jnp.matmul(merged, w_o)


def setup_inputs(seed: int = 0) -> dict:
    key = jax.random.key(seed)
    ks = jax.random.split(key, 24)

    def nrm(k, shape, scale):
        return jax.random.normal(k, shape, jnp.float32) * scale

    dt = jnp.exp(jax.random.uniform(ks[13], (DEPTH, N_DIR, GDN_HEADS), jnp.float32, np.log(1e-3), np.log(1e-1)))
    return {
        'x': nrm(ks[0], (BATCH, SEQ, D_MODEL), 1.0),
        'c': nrm(ks[1], (BATCH, D_MODEL), 1.0),
        'ctx': nrm(ks[2], (BATCH, CTX_LEN, D_MODEL), 1.0),
        'c_ctx': nrm(ks[3], (D_MODEL,), 1.0),
        'w_mod': nrm(ks[4], (DEPTH, D_MODEL, 3 * D_MODEL), 0.5 * D_MODEL ** -0.5),
        'b_mod': nrm(ks[5], (DEPTH, 3 * D_MODEL), 0.02),
        'norm_g': 1.0 + nrm(ks[6], (DEPTH, D_MODEL), 0.05),
        'w_in': nrm(ks[7], (DEPTH, D_MODEL, D_IN), D_MODEL ** -0.5),
        'gla_up': nrm(ks[8], (DEPTH, N_DIR, GLA_RANK, GLA_QK), GLA_RANK ** -0.5),
        'gla_ub': nrm(ks[9], (DEPTH, N_DIR, GLA_QK), 0.1),
        'gla_onorm': 1.0 + nrm(ks[10], (DEPTH, GLA_DV), 0.05),
        'gdn_conv': nrm(ks[11], (DEPTH, CONV_K, CONV_K, CONV_CH), 1.0 / CONV_K),
        'gdn_a_log': jnp.log(jax.random.uniform(ks[12], (DEPTH, N_DIR, GDN_HEADS), jnp.float32, 1.0, 16.0)),
        'gdn_dt_bias': dt + jnp.log(-jnp.expm1(-dt)),
        'gdn_onorm': 1.0 + nrm(ks[14], (DEPTH, GDN_DV), 0.05),
        'w_gla_out': nrm(ks[15], (DEPTH, GLA_V, D_MODEL), GLA_V ** -0.5),
        'w_gdn_out': nrm(ks[16], (DEPTH, GDN_V, D_MODEL), GDN_V ** -0.5),
        'b_gate': nrm(ks[17], (DEPTH, 2 * D_MODEL), 0.1),
        'w_o': nrm(ks[18], (DEPTH, D_MODEL, D_MODEL), D_MODEL ** -0.5),
        'final_g': 1.0 + nrm(ks[19], (D_MODEL,), 0.05),
    }


def reference(x, c, ctx, c_ctx, w_mod, b_mod, norm_g, w_in, gla_up, gla_ub, gla_onorm, gdn_conv,
              gdn_a_log, gdn_dt_bias, gdn_onorm, w_gla_out, w_gdn_out, b_gate, w_o, final_g):
    dtype = x.dtype
    b = x.shape[0]
    f32 = jnp.float32
    for l in range(DEPTH):
        mod_x = jnp.matmul(jax.nn.silu(c), w_mod[l]) + b_mod[l]
        mod_c = jnp.matmul(jax.nn.silu(c_ctx), w_mod[l]) + b_mod[l]
        sh_x, sc_x, gt_x = mod_x[:, :D_MODEL], mod_x[:, D_MODEL:2 * D_MODEL], mod_x[:, 2 * D_MODEL:]
        sh_c, sc_c, gt_c = mod_c[:D_MODEL], mod_c[D_MODEL:2 * D_MODEL], mod_c[2 * D_MODEL:]
        hx = rmsnorm(x, norm_g[l]) * (1.0 + sc_x[:, None, :]) + sh_x[:, None, :]
        hc = rmsnorm(ctx, norm_g[l]) * (1.0 + sc_c) + sh_c

        (caq, cak, cav, cag, calog, cbq, cbk, cbv, cbg, cbeta, cglog, cmg) = branch_inputs(
            hc, w_in[l], gla_up[l], gla_ub[l], gdn_conv[l], gdn_a_log[l], gdn_dt_bias[l], False)
        za = jnp.zeros((b, GLA_HEADS, GLA_DK, GLA_DV), f32)
        zb = jnp.zeros((b, GDN_HEADS, GDN_DK, GDN_DV), f32)
        coa, cob, sa_f, sa_b, sb_f, sb_b = run_mixers(caq, cak, cav, calog, cbq, cbk, cbv, cbeta, cglog, za, za, zb, zb)

        (aq, ak, av, ag, alog, bq, bk, bv, bg, beta, glog, mg) = branch_inputs(
            hx, w_in[l], gla_up[l], gla_ub[l], gdn_conv[l], gdn_a_log[l], gdn_dt_bias[l], True)
        oa, ob, _, _, _, _ = run_mixers(aq, ak, av, alog, bq, bk, bv, beta, glog, sa_f, sa_b, sb_f, sb_b)
        out_x = branch_merge(oa, ob, ag, bg, mg, gla_onorm[l], gdn_onorm[l], w_gla_out[l], w_gdn_out[l], b_gate[l], w_o[l])
        if l < DEPTH - 1:
            out_c = branch_merge(coa, cob, cag, cbg, cmg, gla_onorm[l], gdn_onorm[l], w_gla_out[l], w_gdn_out[l], b_gate[l], w_o[l])
            ctx = ctx + (gt_c * out_c).astype(dtype)
        x = x + (gt_x[:, None, :] * out_x).astype(dtype)
    return rmsnorm(x, final_g)
```

```python
import functools

import jax
import jax.numpy as jnp
from jax import lax
from jax.experimental import pallas as pl
from jax.experimental.pallas import tpu as pltpu

F32 = jnp.float32
BF16 = jnp.bfloat16

CHUNK = 64
N_DIR = 2
GLA_HEADS, GLA_DK, GLA_DV, GLA_RANK, GLA_TAU = 4, 64, 128, 16, 16.0
GLA_QK, GLA_V = GLA_HEADS * GLA_DK, GLA_HEADS * GLA_DV
GDN_HEADS, GDN_DK, GDN_DV = 4, 128, 128
GDN_QK, GDN_V = GDN_HEADS * GDN_DK, GDN_HEADS * GDN_DV
CONV_CH = 2 * GDN_QK + GDN_V
EPS = 1e-6

LANE_ALR = 0
LANE_BETA = N_DIR * GLA_RANK
LANE_DEC = LANE_BETA + N_DIR * GDN_HEADS
SMALL_W = 128

ZA_W = 2 * GLA_QK + GLA_V
ZB_W = CONV_CH
ZG_W = GLA_V + GDN_V

V7X_VMEM_LIMIT = 56 * 1024 * 1024

_NN = (((1,), (0,)), ((), ()))
_NT = (((1,), (1,)), ((), ()))
_TN = (((0,), (0,)), ((), ()))


def _dg(a, b, dims=_NN):
    return lax.dot_general(a, b, dims, preferred_element_type=F32)


def _dot3(a, b, dims=_NN):
    ah = a.astype(BF16)
    al = (a - ah.astype(F32)).astype(BF16)
    bh = b.astype(BF16)
    bl = (b - bh.astype(F32)).astype(BF16)
    return _dg(ah, bh, dims) + _dg(ah, bl, dims) + _dg(al, bh, dims)


def _cum(mask_bf, x, passes):
    acc = None
    rem = x
    for _ in range(passes):
        piece = rem.astype(BF16)
        term = _dg(mask_bf, piece)
        acc = term if acc is None else acc + term
        rem = rem - piece.astype(F32)
    return acc


def _sigmoid(x):
    return 1.0 / (1.0 + jnp.exp(-x))


def _silu(x):
    return x * _sigmoid(x)


def _softplus(x):
    return jnp.maximum(x, 0.0) + jnp.log1p(jnp.exp(-jnp.abs(x)))


def _mod_kernel(c_ref, w_ref, b_ref, o_ref):
    o_ref[...] = _dot3(_silu(c_ref[...]), w_ref[...]) + b_ref[...]


def _mod_call(cc, w_mod, b_mod):
    rows, d = cc.shape
    n = w_mod.shape[1]
    bn = d
    return pl.pallas_call(
        _mod_kernel,
        grid=(n // bn,),
        in_specs=[
            pl.BlockSpec((rows, d), lambda i: (0, 0)),
            pl.BlockSpec((d, bn), lambda i: (0, i)),
            pl.BlockSpec((1, bn), lambda i: (0, i)),
        ],
        out_specs=pl.BlockSpec((rows, bn), lambda i: (0, i)),
        out_shape=jax.ShapeDtypeStruct((rows, n), F32),
        compiler_params=pltpu.CompilerParams(dimension_semantics=("arbitrary",), vmem_limit_bytes=V7X_VMEM_LIMIT),
        name="adaln_mod",
    )(cc, w_mod, b_mod)


def _proj_kernel(x_ref, ctx_ref, mod_ref, g_ref, w_ref, za_ref, zb_ref, zg_ref, zs_ref, *, d_model):
    t = pl.program_id(1)
    xin = jnp.where(t == 0, ctx_ref[...], x_ref[...])
    ms = jnp.mean(xin * xin, axis=-1, keepdims=True)
    y = xin * lax.rsqrt(ms + EPS) * g_ref[...]
    mod = mod_ref[...]
    shift = mod[:, :d_model]
    scale = mod[:, d_model:2 * d_model]
    h = (y * (1.0 + scale) + shift).astype(BF16)
    o0, o1, o2, o3 = 0, ZA_W, ZA_W + ZB_W, ZA_W + ZB_W + ZG_W + 2 * d_model
    za_ref[...] = _dg(h, w_ref[:, o0:o1])
    zb_ref[...] = _dg(h, w_ref[:, o1:o2])
    zs_ref[...] = _dg(h, w_ref[:, o3:o3 + SMALL_W])

    @pl.when(t > 0)
    def _():
        zg_ref[...] = _dg(h, w_ref[:, o2:o3])


def _proj_call(x, ctx, mod3, norm_g, w_r):
    b, seq, d = x.shape
    ctx_len = ctx.shape[1]
    tm = ctx_len
    n_lat = seq // tm
    tot = seq + ctx_len
    zgw = ZG_W + 2 * d
    n_all = w_r.shape[1]

    def row_blk(bi, t):
        return jnp.where(t == 0, n_lat, t - 1)

    kern = functools.partial(_proj_kernel, d_model=d)
    return pl.pallas_call(
        kern,
        grid=(b, n_lat + 1),
        in_specs=[
            pl.BlockSpec((None, tm, d), lambda bi, t: (bi, jnp.maximum(t - 1, 0), 0)),
            pl.BlockSpec((None, tm, d), lambda bi, t: (bi, 0, 0)),
            pl.BlockSpec((None, 1, 3 * d), lambda bi, t: (jnp.where(t == 0, b, bi), 0, 0)),
            pl.BlockSpec((1, d), lambda bi, t: (0, 0)),
            pl.BlockSpec((d, n_all), lambda bi, t: (0, 0), pipeline_mode=pl.Buffered(1)),
        ],
        out_specs=[
            pl.BlockSpec((None, tm, ZA_W), lambda bi, t: (bi, row_blk(bi, t), 0)),
            pl.BlockSpec((None, tm, ZB_W), lambda bi, t: (bi, row_blk(bi, t), 0)),
            pl.BlockSpec((None, tm, zgw), lambda bi, t: (bi, jnp.maximum(t - 1, 0), 0)),
            pl.BlockSpec((None, tm, SMALL_W), lambda bi, t: (bi, row_blk(bi, t), 0)),
        ],
        out_shape=[
            jax.ShapeDtypeStruct((b, tot, ZA_W), F32),
            jax.ShapeDtypeStruct((b, tot, ZB_W), F32),
            jax.ShapeDtypeStruct((b, seq, zgw), F32),
            jax.ShapeDtypeStruct((b, tot, SMALL_W), F32),
        ],
        compiler_params=pltpu.CompilerParams(
            dimension_semantics=("arbitrary", "arbitrary"), vmem_limit_bytes=V7X_VMEM_LIMIT),
        name="in_proj",
    )(x, ctx, mod3, norm_g, w_r)


def _prep_kernel(zp_ref, zc_ref, zn_ref, zs_ref, cw_ref, up_ref, ub_ref, al_ref, dtb_ref,
                 qkv_ref, g_ref, sm_ref, conv_ref, *, n_lat, n_all):
    c = pl.program_id(1)
    is_ctx = c >= n_lat
    row = lax.broadcasted_iota(jnp.int32, (CHUNK, 1), 0)
    w = cw_ref[...]

    @pl.when(jnp.logical_not(is_ctx))
    def _():
        pv = (c > 0).astype(F32)
        nv = (c < n_lat - 1).astype(F32)
        prev, cur, nxt = zp_ref[...], zc_ref[...], zn_ref[...]

        def col(dc):
            return (w[dc:dc + 1] * pv) * prev + w[3 + dc:4 + dc] * cur + (w[6 + dc:7 + dc] * nv) * nxt

        left = jnp.where(row >= 1, pltpu.roll(col(0), 1, 0), 0.0)
        right = jnp.where(row <= CHUNK - 2, pltpu.roll(col(2), CHUNK - 1, 0), 0.0)
        conv_ref[...] = col(1) + left + right

    @pl.when(is_ctx)
    def _():
        pv = (c > n_lat).astype(F32)
        nv = (c < n_all - 1).astype(F32)
        prev, cur, nxt = zp_ref[...], zc_ref[...], zn_ref[...]
        xm = jnp.where(row == CHUNK - 1, prev * pv, cur)
        xp = jnp.where(row == 0, nxt * nv, cur)
        conv_ref[...] = (w[4:5] * cur + pltpu.roll(w[3:4] * xm, 1, 0)
                         + pltpu.roll(w[5:6] * xp, CHUNK - 1, 0))

    a = _silu(conv_ref[...])
    for hh in range(2 * GDN_HEADS):
        sl = slice(hh * GDN_DK, (hh + 1) * GDN_DK)
        xh = a[:, sl]
        inv = lax.rsqrt(jnp.sum(xh * xh, axis=-1, keepdims=True) + EPS)
        scale = GDN_DK ** -0.5 if hh < GDN_HEADS else 1.0
        qkv_ref[:, sl] = xh * (inv * scale)
    qkv_ref[:, 2 * GDN_QK:] = a[:, 2 * GDN_QK:]

    zs = zs_ref[...]
    logits = _dot3(zs, up_ref[...]) + ub_ref[...]
    g_ref[...] = (jnp.minimum(logits, 0.0) - jnp.log1p(jnp.exp(-jnp.abs(logits)))) * (1.0 / GLA_TAU)

    lane = lax.broadcasted_iota(jnp.int32, (1, SMALL_W), 1)
    is_beta = (lane >= LANE_BETA) & (lane < LANE_DEC)
    is_dec = (lane >= LANE_DEC) & (lane < LANE_DEC + N_DIR * GDN_HEADS)
    glog = -jnp.exp(al_ref[...]) * _softplus(zs + dtb_ref[...])
    sm_ref[...] = jnp.where(is_beta, _sigmoid(zs), jnp.where(is_dec, glog, 0.0))


def _prep_call(zb, zs, cw, up_bd, ub, alv, dtbv, n_lat):
    b, tot, _ = zb.shape
    n_all = tot // CHUNK

    def prev_blk(bi, c):
        return jnp.where(c >= n_lat, jnp.maximum(c - 1, n_lat), jnp.maximum(c - 1, 0))

    def next_blk(bi, c):
        return jnp.where(c >= n_lat, jnp.minimum(c + 1, n_all - 1), jnp.minimum(c + 1, n_lat - 1))

    kern = functools.partial(_prep_kernel, n_lat=n_lat, n_all=n_all)
    full = lambda shape: pl.BlockSpec(shape, lambda bi, c: (0,) * len(shape))
    return pl.pallas_call(
        kern,
        grid=(b, n_all),
        in_specs=[
            pl.BlockSpec((None, CHUNK, ZB_W), lambda bi, c: (bi, prev_blk(bi, c), 0)),
            pl.BlockSpec((None, CHUNK, ZB_W), lambda bi, c: (bi, c, 0)),
            pl.BlockSpec((None, CHUNK, ZB_W), lambda bi, c: (bi, next_blk(bi, c), 0)),
            pl.BlockSpec((None, CHUNK, SMALL_W), lambda bi, c: (bi, c, 0)),
            full(cw.shape), full(up_bd.shape), full(ub.shape), full(alv.shape), full(dtbv.shape),
        ],
        out_specs=[
            pl.BlockSpec((None, CHUNK, ZB_W), lambda bi, c: (bi, c, 0)),
            pl.BlockSpec((None, CHUNK, N_DIR * GLA_QK), lambda bi, c: (bi, c, 0)),
            pl.BlockSpec((None, CHUNK, SMALL_W), lambda bi, c: (bi, c, 0)),
        ],
        out_shape=[
            jax.ShapeDtypeStruct((b, tot, ZB_W), F32),
            jax.ShapeDtypeStruct((b, tot, N_DIR * GLA_QK), F32),
            jax.ShapeDtypeStruct((b, tot, SMALL_W), F32),
        ],
        scratch_shapes=[pltpu.VMEM((CHUNK, ZB_W), F32)],
        compiler_params=pltpu.CompilerParams(
            dimension_semantics=("arbitrary", "arbitrary"), vmem_limit_bytes=V7X_VMEM_LIMIT),
        name="mixer_prep",
    )(zb, zb, zb, zs, cw, up_bd, ub, alv, dtbv)


def _tri_inverse(a_strict, eye_f):
    n = -a_strict
    p = eye_f + n
    steps = max(1, (CHUNK - 1).bit_length()) - 1
    for _ in range(steps):
        nb = n.astype(BF16)
        n = _dg(nb, nb)
        p = p + _dg(p.astype(BF16), n.astype(BF16))
    return p


def _mix_kernel(za_ref, g_ref, qkv_ref, sm_ref, o_ref, sa_ref, sb_ref):
    d = pl.program_id(1)
    j = pl.program_id(2)

    @pl.when(j == 0)
    def _():
        sa_ref[...] = jnp.zeros_like(sa_ref)
        sb_ref[...] = jnp.zeros_like(sb_ref)

    fwd = d == 0
    r = lax.broadcasted_iota(jnp.int32, (CHUNK, CHUNK), 0)
    cc = lax.broadcasted_iota(jnp.int32, (CHUNK, CHUNK), 1)
    lo = jnp.where(fwd, cc, r)
    hi = jnp.where(fwd, r, cc)
    incl = lo <= hi
    strict = lo < hi
    eye = cc == r
    eye_f = eye.astype(F32)
    cum_bf = incl.astype(BF16)
    last = lambda x: jnp.where(fwd, x[CHUNK - 1:CHUNK], x[0:1])

    za = za_ref[...]
    q = za[:, :GLA_QK] * (GLA_DK ** -0.5)
    k = za[:, GLA_QK:2 * GLA_QK]
    v = za[:, 2 * GLA_QK:]
    bcum = _cum(cum_bf, g_ref[...], 2)
    bl = last(bcum)
    qe = (q * jnp.exp(bcum)).astype(BF16)
    kd = (k * jnp.exp(-bcum)).astype(BF16)
    kl = (k * jnp.exp(bl - bcum)).astype(BF16)
    ebl = jnp.exp(bl)
    for h in range(GLA_HEADS):
        sk = slice(h * GLA_DK, (h + 1) * GLA_DK)
        sv = slice(h * GLA_DV, (h + 1) * GLA_DV)
        v_h = v[:, sv].astype(BF16)
        att = jnp.where(incl, _dg(qe[:, sk], kd[:, sk], _NT), 0.0).astype(BF16)
        s_old = sa_ref[h]
        o_ref[:, sv] = _dg(qe[:, sk], s_old.astype(BF16)) + _dg(att, v_h)
        ecol = jnp.sum(jnp.where(eye, ebl[:, sk], 0.0), axis=1, keepdims=True)
        sa_ref[h] = s_old * ecol + _dg(kl[:, sk], v_h, _TN)

    qkv = qkv_ref[...]
    sm = sm_ref[...]
    gam_all = _cum(cum_bf, sm, 3)
    gam_t = gam_all.T
    for h in range(GDN_HEADS):
        lf, lb = LANE_DEC + h, LANE_DEC + GDN_HEADS + h
        bf_, bb_ = LANE_BETA + h, LANE_BETA + GDN_HEADS + h
        gam_c = jnp.where(fwd, gam_all[:, lf:lf + 1], gam_all[:, lb:lb + 1])
        gam_r = jnp.where(fwd, gam_t[lf:lf + 1, :], gam_t[lb:lb + 1, :])
        beta_c = jnp.where(fwd, sm[:, bf_:bf_ + 1], sm[:, bb_:bb_ + 1])
        gl = last(gam_c)
        dm = jnp.where(incl, jnp.exp(jnp.minimum(gam_c - gam_r, 0.0)), 0.0)
        q_h = qkv[:, h * GDN_DK:(h + 1) * GDN_DK]
        k_h = qkv[:, GDN_QK + h * GDN_DK:GDN_QK + (h + 1) * GDN_DK]
        v_h = qkv[:, 2 * GDN_QK + h * GDN_DV:2 * GDN_QK + (h + 1) * GDN_DV]
        kb = k_h * beta_c
        vb = v_h * beta_c
        k_bf = k_h.astype(BF16)
        kk = _dg(kb.astype(BF16), k_bf, _NT)
        qk = _dg(q_h.astype(BF16), k_bf, _NT)
        a_low = jnp.where(strict, kk * dm, 0.0)
        aqk = (qk * dm).astype(BF16)
        tmat = _tri_inverse(a_low, eye_f).astype(BF16)
        egc = jnp.exp(gam_c)
        u = _dg(tmat, vb.astype(BF16))
        w = _dg(tmat, (kb * egc).astype(BF16))
        s_old = sb_ref[h]
        s_bf = s_old.astype(BF16)
        v_new = u - _dg(w.astype(BF16), s_bf)
        vn_bf = v_new.astype(BF16)
        o_ref[:, GLA_V + h * GDN_DV:GLA_V + (h + 1) * GDN_DV] = (
            _dg((q_h * egc).astype(BF16), s_bf) + _dg(aqk, vn_bf))
        khat = (k_h * jnp.exp(gl - gam_c)).astype(BF16)
        sb_ref[h] = s_old * jnp.exp(gl) + _dg(khat, vn_bf, _TN)


def _mix_call(za, g, qkv, sm, n_lat):
    b, tot, _ = za.shape
    n_all = tot // CHUNK
    n_ctx = n_all - n_lat

    def in_blk(dd, j):
        fwd_c = jnp.where(j < n_ctx, n_lat + j, j - n_ctx)
        return jnp.where(dd == 0, fwd_c, n_all - 1 - j)

    def out_blk(dd, j):
        jj = jnp.maximum(j, n_ctx)
        return jnp.where(dd == 0, jj - n_ctx, n_all - 1 - jj)

    return pl.pallas_call(
        _mix_kernel,
        grid=(b, N_DIR, n_all),
        in_specs=[
            pl.BlockSpec((None, CHUNK, ZA_W), lambda bi, dd, j: (bi, in_blk(dd, j), 0)),
            pl.BlockSpec((None, CHUNK, GLA_QK), lambda bi, dd, j: (bi, in_blk(dd, j), dd)),
            pl.BlockSpec((None, CHUNK, ZB_W), lambda bi, dd, j: (bi, in_blk(dd, j), 0)),
            pl.BlockSpec((None, CHUNK, SMALL_W), lambda bi, dd, j: (bi, in_blk(dd, j), 0)),
        ],
        out_specs=pl.BlockSpec((None, None, CHUNK, GLA_V + GDN_V), lambda bi, dd, j: (bi, dd, out_blk(dd, j), 0)),
        out_shape=jax.ShapeDtypeStruct((b, N_DIR, n_lat * CHUNK, GLA_V + GDN_V), F32),
        scratch_shapes=[
            pltpu.VMEM((GLA_HEADS, GLA_DK, GLA_DV), F32),
            pltpu.VMEM((GDN_HEADS, GDN_DK, GDN_DV), F32),
        ],
        compiler_params=pltpu.CompilerParams(
            dimension_semantics=("arbitrary", "arbitrary", "arbitrary"), vmem_limit_bytes=V7X_VMEM_LIMIT),
        name="mixer_scan",
    )(za, g, qkv, sm)


def _merge_kernel(o_ref, zg_ref, x_ref, mod_ref, wa_ref, wb_ref, wo_ref, ga_ref, gb_ref, bg_ref, fg_ref,
                  y_ref, *, d_model):
    o = o_ref[0] + o_ref[1]
    zg = zg_ref[...]

    def head_norm(xh, gain):
        ms = jnp.mean(xh * xh, axis=-1, keepdims=True)
        return xh * lax.rsqrt(ms + EPS) * gain

    ya = jnp.concatenate(
        [head_norm(o[:, h * GLA_DV:(h + 1) * GLA_DV], ga_ref[...]) for h in range(GLA_HEADS)], axis=1)
    yb = jnp.concatenate(
        [head_norm(o[:, GLA_V + h * GDN_DV:GLA_V + (h + 1) * GDN_DV], gb_ref[...]) for h in range(GDN_HEADS)],
        axis=1)
    ya = ya * _silu(zg[:, :GLA_V])
    yb = yb * _silu(zg[:, GLA_V:GLA_V + GDN_V])
    pa = _dg(ya.astype(BF16), wa_ref[...])
    pb = _dg(yb.astype(BF16), wb_ref[...])
    gates = _sigmoid(zg[:, ZG_W:] + bg_ref[...])
    merged = gates[:, :d_model] * pa + gates[:, d_model:] * pb
    out = _dg(merged.astype(BF16), wo_ref[...])
    gate = mod_ref[...][:, 2 * d_model:]
    xn = x_ref[...] + gate * out
    ms = jnp.mean(xn * xn, axis=-1, keepdims=True)
    y_ref[...] = xn * lax.rsqrt(ms + EPS) * fg_ref[...]


def _merge_call(o, zg, x, mod3, wa, wb, wo, ga, gb, bg, fg):
    b, seq, d = x.shape
    tm = 256
    kern = functools.partial(_merge_kernel, d_model=d)
    full = lambda a: pl.BlockSpec(a.shape, lambda bi, t: (0,) * a.ndim)
    return pl.pallas_call(
        kern,
        grid=(b, seq // tm),
        in_specs=[
            pl.BlockSpec((None, N_DIR, tm, GLA_V + GDN_V), lambda bi, t: (bi, 0, t, 0)),
            pl.BlockSpec((None, tm, zg.shape[2]), lambda bi, t: (bi, t, 0)),
            pl.BlockSpec((None, tm, d), lambda bi, t: (bi, t, 0)),
            pl.BlockSpec((None, 1, 3 * d), lambda bi, t: (bi, 0, 0)),
            full(wa), full(wb), full(wo), full(ga), full(gb), full(bg), full(fg),
        ],
        out_specs=pl.BlockSpec((None, tm, d), lambda bi, t: (bi, t, 0)),
        out_shape=jax.ShapeDtypeStruct((b, seq, d), x.dtype),
        compiler_params=pltpu.CompilerParams(
            dimension_semantics=("arbitrary", "arbitrary"), vmem_limit_bytes=V7X_VMEM_LIMIT),
        name="merge_out",
    )(o, zg, x, mod3, wa, wb, wo, ga, gb, bg, fg)


def _regroup_w_in(w):
    d = w.shape[0]
    o = 0
    offs = []
    for size in (GLA_QK, GLA_QK, GLA_V, GLA_V, N_DIR * GLA_RANK, GDN_QK, GDN_QK, GDN_V, GDN_V,
                 N_DIR * GDN_HEADS, N_DIR * GDN_HEADS, 2 * d):
        offs.append((o, o + size))
        o += size
    aq, ak, av, ag, alr, bq, bk, bv, bg, bbeta, bdec, mg = [w[:, s:e] for s, e in offs]
    small = jnp.concatenate([alr, bbeta, bdec], axis=1)
    small = jnp.pad(small, ((0, 0), (0, SMALL_W - small.shape[1])))
    return jnp.concatenate([aq, ak, av, bq, bk, bv, ag, bg, mg, small], axis=1).astype(BF16)


def kernel(x, c, ctx, c_ctx, w_mod, b_mod, norm_g, w_in, gla_up, gla_ub, gla_onorm, gdn_conv, gdn_a_log,
           gdn_dt_bias, gdn_onorm, w_gla_out, w_gdn_out, b_gate, w_o, final_g):
    b, seq, d = x.shape
    assert w_mod.shape[0] == 1, "single-layer block"
    assert seq % CHUNK == 0 and ctx.shape[1] % CHUNK == 0 and seq % ctx.shape[1] == 0
    n_lat = seq // CHUNK

    rows = -(-(b + 1) // 8) * 8
    cc = jnp.zeros((rows, d), F32).at[:b].set(c).at[b].set(c_ctx)
    mod = _mod_call(cc, w_mod[0], b_mod[0][None, :])
    mod3 = mod[:, None, :]

    w_r = _regroup_w_in(w_in[0])
    za, zb, zg, zs = _proj_call(x, ctx, mod3, norm_g[0][None, :], w_r)

    cw = gdn_conv[0].reshape(9, CONV_CH)
    up_bd = jnp.zeros((SMALL_W, N_DIR * GLA_QK), F32)
    for n in range(N_DIR):
        up_bd = up_bd.at[n * GLA_RANK:(n + 1) * GLA_RANK, n * GLA_QK:(n + 1) * GLA_QK].set(gla_up[0, n])
    ub = gla_ub[0].reshape(1, N_DIR * GLA_QK)
    nd = N_DIR * GDN_HEADS
    alv = jnp.zeros((1, SMALL_W), F32).at[0, LANE_DEC:LANE_DEC + nd].set(gdn_a_log[0].reshape(nd))
    dtbv = jnp.zeros((1, SMALL_W), F32).at[0, LANE_DEC:LANE_DEC + nd].set(gdn_dt_bias[0].reshape(nd))
    qkv, g, sm = _prep_call(zb, zs, cw, up_bd, ub, alv, dtbv, n_lat)

    o = _mix_call(za, g, qkv, sm, n_lat)

    ga = jnp.tile(gla_onorm[0][None, :], (1, 1))
    gb = jnp.tile(gdn_onorm[0][None, :], (1, 1))
    return _merge_call(o, zg, x, mod3, w_gla_out[0].astype(BF16), w_gdn_out[0].astype(BF16),
                       w_o[0].astype(BF16), ga, gb, b_gate[0][None, :], final_g[None, :])
```

```python
import functools

import jax
import jax.numpy as jnp
from jax import lax
from jax.experimental import pallas as pl
from jax.experimental.pallas import tpu as pltpu

F32 = jnp.float32
BF16 = jnp.bfloat16

CHUNK = 64
N_DIR = 2
GLA_HEADS, GLA_DK, GLA_DV, GLA_RANK, GLA_TAU = 4, 64, 128, 16, 16.0
GLA_QK, GLA_V = GLA_HEADS * GLA_DK, GLA_HEADS * GLA_DV
GDN_HEADS, GDN_DK, GDN_DV = 4, 128, 128
GDN_QK, GDN_V = GDN_HEADS * GDN_DK, GDN_HEADS * GDN_DV
CONV_CH = 2 * GDN_QK + GDN_V
EPS = 1e-6
O_W = GLA_V + GDN_V

LANE_ALR = 0
LANE_BETA = N_DIR * GLA_RANK
LANE_DEC = LANE_BETA + N_DIR * GDN_HEADS
SMALL_W = 128

ZA_W = 2 * GLA_QK + GLA_V
ZB_W = CONV_CH
ZG_W = GLA_V + GDN_V

V7X_VMEM_LIMIT = 56 * 1024 * 1024

_NN = (((1,), (0,)), ((), ()))
_NT = (((1,), (1,)), ((), ()))
_TN = (((0,), (0,)), ((), ()))


def _dg(a, b, dims=_NN):
    return lax.dot_general(a, b, dims, preferred_element_type=F32)


def _dot3(a, b, dims=_NN):
    ah = a.astype(BF16)
    al = (a - ah.astype(F32)).astype(BF16)
    bh = b.astype(BF16)
    bl = (b - bh.astype(F32)).astype(BF16)
    return _dg(ah, bh, dims) + _dg(ah, bl, dims) + _dg(al, bh, dims)


def _cum(mask_bf, x, passes):
    acc = None
    rem = x
    for _ in range(passes):
        piece = rem.astype(BF16)
        term = _dg(mask_bf, piece)
        acc = term if acc is None else acc + term
        rem = rem - piece.astype(F32)
    return acc


def _sigmoid(x):
    return 1.0 / (1.0 + jnp.exp(-x))


def _silu(x):
    return x * _sigmoid(x)


def _softplus(x):
    return jnp.maximum(x, 0.0) + jnp.log1p(jnp.exp(-jnp.abs(x)))


def _mod_kernel(c_ref, w_ref, b_ref, o_ref):
    o_ref[...] = _dot3(_silu(c_ref[...]), w_ref[...]) + b_ref[...]


def _mod_call(cc, w_mod, b_mod):
    rows, d = cc.shape
    n = w_mod.shape[1]
    bn = d
    return pl.pallas_call(
        _mod_kernel,
        grid=(n // bn,),
        in_specs=[
            pl.BlockSpec((rows, d), lambda i: (0, 0)),
            pl.BlockSpec((d, bn), lambda i: (0, i)),
            pl.BlockSpec((1, bn), lambda i: (0, i)),
        ],
        out_specs=pl.BlockSpec((rows, bn), lambda i: (0, i)),
        out_shape=jax.ShapeDtypeStruct((rows, n), F32),
        compiler_params=pltpu.CompilerParams(dimension_semantics=("arbitrary",), vmem_limit_bytes=V7X_VMEM_LIMIT),
        name="adaln_mod",
    )(cc, w_mod, b_mod)


def _proj_kernel(x_ref, ctx_ref, mod_ref, g_ref, w_ref, za_ref, zb_ref, zg_ref, zs_ref, *, d_model):
    t = pl.program_id(1)
    xin = jnp.where(t == 0, ctx_ref[...], x_ref[...])
    ms = jnp.mean(xin * xin, axis=-1, keepdims=True)
    y = xin * lax.rsqrt(ms + EPS) * g_ref[...]
    mod = mod_ref[...]
    shift = mod[:, :d_model]
    scale = mod[:, d_model:2 * d_model]
    h = (y * (1.0 + scale) + shift).astype(BF16)
    o0, o1, o2, o3 = 0, ZA_W, ZA_W + ZB_W, ZA_W + ZB_W + ZG_W + 2 * d_model
    za_ref[...] = _dg(h, w_ref[:, o0:o1])
    zb_ref[...] = _dg(h, w_ref[:, o1:o2])
    zs_ref[...] = _dg(h, w_ref[:, o3:o3 + SMALL_W])

    @pl.when(t > 0)
    def _():
        zg_ref[...] = _dg(h, w_ref[:, o2:o3])


def _proj_call(x, ctx, mod3, norm_g, w_r):
    b, seq, d = x.shape
    ctx_len = ctx.shape[1]
    tm = ctx_len
    n_lat = seq // tm
    tot = seq + ctx_len
    zgw = ZG_W + 2 * d
    n_all = w_r.shape[1]

    def row_blk(bi, t):
        return jnp.where(t == 0, n_lat, t - 1)

    kern = functools.partial(_proj_kernel, d_model=d)
    return pl.pallas_call(
        kern,
        grid=(b, n_lat + 1),
        in_specs=[
            pl.BlockSpec((None, tm, d), lambda bi, t: (bi, jnp.maximum(t - 1, 0), 0)),
            pl.BlockSpec((None, tm, d), lambda bi, t: (bi, 0, 0)),
            pl.BlockSpec((None, 1, 3 * d), lambda bi, t: (jnp.where(t == 0, b, bi), 0, 0)),
            pl.BlockSpec((1, d), lambda bi, t: (0, 0)),
            pl.BlockSpec((d, n_all), lambda bi, t: (0, 0), pipeline_mode=pl.Buffered(1)),
        ],
        out_specs=[
            pl.BlockSpec((None, tm, ZA_W), lambda bi, t: (bi, row_blk(bi, t), 0)),
            pl.BlockSpec((None, tm, ZB_W), lambda bi, t: (bi, row_blk(bi, t), 0)),
            pl.BlockSpec((None, tm, zgw), lambda bi, t: (bi, jnp.maximum(t - 1, 0), 0)),
            pl.BlockSpec((None, tm, SMALL_W), lambda bi, t: (bi, row_blk(bi, t), 0)),
        ],
        out_shape=[
            jax.ShapeDtypeStruct((b, tot, ZA_W), F32),
            jax.ShapeDtypeStruct((b, tot, ZB_W), F32),
            jax.ShapeDtypeStruct((b, seq, zgw), F32),
            jax.ShapeDtypeStruct((b, tot, SMALL_W), F32),
        ],
        compiler_params=pltpu.CompilerParams(
            dimension_semantics=("arbitrary", "arbitrary"), vmem_limit_bytes=V7X_VMEM_LIMIT),
        name="in_proj",
    )(x, ctx, mod3, norm_g, w_r)


def _tri_inverse_many(a_list, eye_f):
    ns = [-a for a in a_list]
    ps = [eye_f + n for n in ns]
    steps = max(1, (CHUNK - 1).bit_length()) - 1
    for _ in range(steps):
        nbs = [n.astype(BF16) for n in ns]
        ns = [_dg(nb, nb) for nb in nbs]
        nbs = [n.astype(BF16) for n in ns]
        ps = [p + _dg(p.astype(BF16), nb) for p, nb in zip(ps, nbs)]
    return ps


def _pre_kernel(zp_ref, zc_ref, zn_ref, za_ref, zs_ref, cw_ref, up_ref, ub_ref, al_ref, dtb_ref,
                lhs_ref, cg_ref, eg_ref, qe_ref, ca_ref, er_ref, oi_ref, conv_ref, *, n_lat, n_all):
    c = pl.program_id(1)
    is_ctx = c >= n_lat
    row = lax.broadcasted_iota(jnp.int32, (CHUNK, 1), 0)
    w = cw_ref[...]

    @pl.when(jnp.logical_not(is_ctx))
    def _():
        pv = (c > 0).astype(F32)
        nv = (c < n_lat - 1).astype(F32)
        prev, cur, nxt = zp_ref[...], zc_ref[...], zn_ref[...]

        def col(dc):
            return (w[dc:dc + 1] * pv) * prev + w[3 + dc:4 + dc] * cur + (w[6 + dc:7 + dc] * nv) * nxt

        left = jnp.where(row >= 1, pltpu.roll(col(0), 1, 0), 0.0)
        right = jnp.where(row <= CHUNK - 2, pltpu.roll(col(2), CHUNK - 1, 0), 0.0)
        conv_ref[...] = col(1) + left + right

    @pl.when(is_ctx)
    def _():
        pv = (c > n_lat).astype(F32)
        nv = (c < n_all - 1).astype(F32)
        prev, cur, nxt = zp_ref[...], zc_ref[...], zn_ref[...]
        xm = jnp.where(row == CHUNK - 1, prev * pv, cur)
        xp = jnp.where(row == 0, nxt * nv, cur)
        conv_ref[...] = (w[4:5] * cur + pltpu.roll(w[3:4] * xm, 1, 0)
                         + pltpu.roll(w[5:6] * xp, CHUNK - 1, 0))

    dirs = range(N_DIR)
    r = lax.broadcasted_iota(jnp.int32, (CHUNK, CHUNK), 0)
    cc = lax.broadcasted_iota(jnp.int32, (CHUNK, CHUNK), 1)
    incl = [cc <= r, cc >= r]
    strict = [cc < r, cc > r]
    eye_f = (cc == r).astype(F32)
    cum_bf = [m.astype(BF16) for m in incl]
    last = [lambda x: x[CHUNK - 1:CHUNK], lambda x: x[0:1]]

    a = _silu(conv_ref[...])
    qs, ks, vs = [], [], []
    for hh in range(2 * GDN_HEADS):
        xh = a[:, hh * GDN_DK:(hh + 1) * GDN_DK]
        inv = lax.rsqrt(jnp.sum(xh * xh, axis=-1, keepdims=True) + EPS)
        if hh < GDN_HEADS:
            qs.append(xh * (inv * GDN_DK ** -0.5))
        else:
            ks.append(xh * inv)
    for h in range(GDN_HEADS):
        vs.append(a[:, 2 * GDN_QK + h * GDN_DV:2 * GDN_QK + (h + 1) * GDN_DV])

    zs = zs_ref[...]
    logits = _dot3(zs, up_ref[...]) + ub_ref[...]
    g_all = (jnp.minimum(logits, 0.0) - jnp.log1p(jnp.exp(-jnp.abs(logits)))) * (1.0 / GLA_TAU)
    beta_all = _sigmoid(zs)
    glog_all = -jnp.exp(al_ref[...]) * _softplus(zs + dtb_ref[...])

    za = za_ref[...]
    q = za[:, :GLA_QK] * (GLA_DK ** -0.5)
    k = za[:, GLA_QK:2 * GLA_QK]
    v_bf = [za[:, 2 * GLA_QK + h * GLA_DV:2 * GLA_QK + (h + 1) * GLA_DV].astype(BF16) for h in range(GLA_HEADS)]
    bcum = [_cum(cum_bf[d], g_all[:, d * GLA_QK:(d + 1) * GLA_QK], 2) for d in dirs]
    bl = [last[d](bcum[d]) for d in dirs]
    qe = [(q * jnp.exp(bcum[d])).astype(BF16) for d in dirs]
    kd = [(k * jnp.exp(-bcum[d])).astype(BF16) for d in dirs]
    kl = [(k * jnp.exp(bl[d] - bcum[d])).astype(BF16) for d in dirs]
    hd_a = [(d, h) for d in dirs for h in range(GLA_HEADS)]
    sk = lambda h: slice(h * GLA_DK, (h + 1) * GLA_DK)
    att = [jnp.where(incl[d], _dg(qe[d][:, sk(h)], kd[d][:, sk(h)], _NT), 0.0).astype(BF16) for d, h in hd_a]
    oi_a = [_dg(att[i], v_bf[h]) for i, (d, h) in enumerate(hd_a)]
    c_a = [_dg(kl[d][:, sk(h)], v_bf[h], _TN) for d, h in hd_a]
    for d in dirs:
        qe_ref[d] = qe[d]
        er_ref[d] = jnp.exp(bl[d])
    for i, (d, h) in enumerate(hd_a):
        ca_ref[d, :, h * GLA_DV:(h + 1) * GLA_DV] = c_a[i].astype(BF16)

    gam_all = [_cum(cum_bf[d], glog_all, 3) for d in dirs]
    gam_t = [g.T for g in gam_all]
    k_bf = [kh.astype(BF16) for kh in ks]
    raw = [_dg(jnp.concatenate([ks[h], qs[h]], axis=0).astype(BF16), k_bf[h], _NT) for h in range(GDN_HEADS)]
    hd_b = [(d, h) for d in dirs for h in range(GDN_HEADS)]
    gam_c, beta_c, gl, dm = [], [], [], []
    for d, h in hd_b:
        ld, lb = LANE_DEC + d * GDN_HEADS + h, LANE_BETA + d * GDN_HEADS + h
        gc = gam_all[d][:, ld:ld + 1]
        gr = gam_t[d][ld:ld + 1, :]
        gam_c.append(gc)
        beta_c.append(beta_all[:, lb:lb + 1])
        gl.append(last[d](gc))
        dm.append(jnp.where(incl[d], jnp.exp(jnp.minimum(gc - gr, 0.0)), 0.0))
    a_low = [jnp.where(strict[d], raw[h][:CHUNK] * beta_c[i] * dm[i], 0.0) for i, (d, h) in enumerate(hd_b)]
    aqk = [(raw[h][CHUNK:] * dm[i]).astype(BF16) for i, (d, h) in enumerate(hd_b)]
    tmat = _tri_inverse_many(a_low, eye_f)
    egc = [jnp.exp(gc) for gc in gam_c]
    rhs = [jnp.concatenate([vs[h] * beta_c[i], ks[h] * (beta_c[i] * egc[i])], axis=1).astype(BF16)
           for i, (d, h) in enumerate(hd_b)]
    uw = [_dg(tmat[i].astype(BF16), rhs[i]).astype(BF16) for i in range(len(hd_b))]
    aw = [_dg(aqk[i], uw[i]) for i in range(len(hd_b))]
    khat = [(ks[h] * jnp.exp(gl[i] - gam_c[i])).astype(BF16) for i, (d, h) in enumerate(hd_b)]
    pc = [_dg(khat[i], uw[i], _TN) for i in range(len(hd_b))]
    lane = lax.broadcasted_iota(jnp.int32, (1, SMALL_W), 1)
    for d in dirs:
        row_e = jnp.zeros((1, SMALL_W), F32)
        for h in range(GDN_HEADS):
            row_e = jnp.where(lane == h, jnp.exp(gl[d * GDN_HEADS + h]), row_e)
        eg_ref[d] = row_e
    for i, (d, h) in enumerate(hd_b):
        lhs_ref[d, h, :GDN_DK] = pc[i][:, GDN_DV:].astype(BF16)
        lhs_ref[d, h, GDN_DK:] = (qs[h] * egc[i] - aw[i][:, GDN_DV:]).astype(BF16)
        cg_ref[d, h] = pc[i][:, :GDN_DV].astype(BF16)

    @pl.when(jnp.logical_not(is_ctx))
    def _():
        for h in range(GLA_HEADS):
            oi_ref[:, h * GLA_DV:(h + 1) * GLA_DV] = oi_a[h] + oi_a[GLA_HEADS + h]
        for h in range(GDN_HEADS):
            oi_ref[:, GLA_V + h * GDN_DV:GLA_V + (h + 1) * GDN_DV] = (
                aw[h][:, :GDN_DV] + aw[GDN_HEADS + h][:, :GDN_DV])


def _pre_call(zb, za, zs, cw, up_bd, ub, alv, dtbv, n_lat):
    b, tot, _ = zb.shape
    n_all = tot // CHUNK

    def prev_blk(bi, c):
        return jnp.where(c >= n_lat, jnp.maximum(c - 1, n_lat), jnp.maximum(c - 1, 0))

    def next_blk(bi, c):
        return jnp.where(c >= n_lat, jnp.minimum(c + 1, n_all - 1), jnp.minimum(c + 1, n_lat - 1))

    kern = functools.partial(_pre_kernel, n_lat=n_lat, n_all=n_all)
    full = lambda shape: pl.BlockSpec(shape, lambda bi, c: (0,) * len(shape))
    per_chunk = lambda *tail: pl.BlockSpec((None, None) + tail, lambda bi, c: (bi, c) + (0,) * len(tail))
    return pl.pallas_call(
        kern,
        grid=(b, n_all),
        in_specs=[
            pl.BlockSpec((None, CHUNK, ZB_W), lambda bi, c: (bi, prev_blk(bi, c), 0)),
            pl.BlockSpec((None, CHUNK, ZB_W), lambda bi, c: (bi, c, 0)),
            pl.BlockSpec((None, CHUNK, ZB_W), lambda bi, c: (bi, next_blk(bi, c), 0)),
            pl.BlockSpec((None, CHUNK, ZA_W), lambda bi, c: (bi, c, 0)),
            pl.BlockSpec((None, CHUNK, SMALL_W), lambda bi, c: (bi, c, 0)),
            full(cw.shape), full(up_bd.shape), full(ub.shape), full(alv.shape), full(dtbv.shape),
        ],
        out_specs=[
            per_chunk(N_DIR, GDN_HEADS, GDN_DK + CHUNK, GDN_DV),
            per_chunk(N_DIR, GDN_HEADS, GDN_DK, GDN_DV),
            per_chunk(N_DIR, 1, SMALL_W),
            per_chunk(N_DIR, CHUNK, GLA_QK),
            per_chunk(N_DIR, CHUNK, GLA_V),
            per_chunk(N_DIR, 1, GLA_QK),
            pl.BlockSpec((None, CHUNK, O_W), lambda bi, c: (bi, jnp.minimum(c, n_lat - 1), 0)),
        ],
        out_shape=[
            jax.ShapeDtypeStruct((b, n_all, N_DIR, GDN_HEADS, GDN_DK + CHUNK, GDN_DV), BF16),
            jax.ShapeDtypeStruct((b, n_all, N_DIR, GDN_HEADS, GDN_DK, GDN_DV), BF16),
            jax.ShapeDtypeStruct((b, n_all, N_DIR, 1, SMALL_W), F32),
            jax.ShapeDtypeStruct((b, n_all, N_DIR, CHUNK, GLA_QK), BF16),
            jax.ShapeDtypeStruct((b, n_all, N_DIR, CHUNK, GLA_V), BF16),
            jax.ShapeDtypeStruct((b, n_all, N_DIR, 1, GLA_QK), F32),
            jax.ShapeDtypeStruct((b, n_lat * CHUNK, O_W), F32),
        ],
        scratch_shapes=[pltpu.VMEM((CHUNK, ZB_W), F32)],
        compiler_params=pltpu.CompilerParams(
            dimension_semantics=("arbitrary", "arbitrary"), vmem_limit_bytes=V7X_VMEM_LIMIT),
        name="chunk_pre",
    )(zb, zb, zb, za, zs, cw, up_bd, ub, alv, dtbv)


def _scan_kernel(*refs):
    ins, (of_ref, ob_ref, sa_ref, sb_ref) = refs[:12], refs[12:]
    per_dir = [ins[0:6], ins[6:12]]
    o_refs = [of_ref, ob_ref]
    j = pl.program_id(1)

    @pl.when(j == 0)
    def _():
        sa_ref[...] = jnp.zeros_like(sa_ref)
        sb_ref[...] = jnp.zeros_like(sb_ref)

    r = lax.broadcasted_iota(jnp.int32, (GLA_DK, GLA_DK), 0)
    cc = lax.broadcasted_iota(jnp.int32, (GLA_DK, GLA_DK), 1)
    eye = cc == r

    hd = [(d, h) for d in range(N_DIR) for h in range(GDN_HEADS)]
    s_b = [sb_ref[d, h] for d, h in hd]
    res_b = [_dg(per_dir[d][0][h], s_b[i].astype(BF16)) for i, (d, h) in enumerate(hd)]
    for i, (d, h) in enumerate(hd):
        e = per_dir[d][2][:, h:h + 1]
        sb_ref[d, h] = s_b[i] * e - res_b[i][:GDN_DK] + per_dir[d][1][h].astype(F32)
        o_refs[d][:, GLA_V + h * GDN_DV:GLA_V + (h + 1) * GDN_DV] = res_b[i][GDN_DK:]
    s_a = [sa_ref[d, h] for d, h in hd]
    res_a = [_dg(per_dir[d][3][:, h * GLA_DK:(h + 1) * GLA_DK], s_a[i].astype(BF16)) for i, (d, h) in enumerate(hd)]
    for i, (d, h) in enumerate(hd):
        e_row = per_dir[d][5][:, h * GLA_DK:(h + 1) * GLA_DK]
        ecol = jnp.sum(jnp.where(eye, e_row, 0.0), axis=1, keepdims=True)
        sa_ref[d, h] = s_a[i] * ecol + per_dir[d][4][:, h * GLA_DV:(h + 1) * GLA_DV].astype(F32)
        o_refs[d][:, h * GLA_DV:(h + 1) * GLA_DV] = res_a[i]


def _scan_call(lhs, cg, eg, qe, ca, er, n_lat):
    b, n_all = lhs.shape[0], lhs.shape[1]
    n_ctx = n_all - n_lat

    def in_blk(dd, j):
        if dd == 0:
            return jnp.where(j < n_ctx, n_lat + j, j - n_ctx)
        return n_all - 1 - j

    def out_blk(dd, j):
        jj = jnp.maximum(j, n_ctx)
        return jj - n_ctx if dd == 0 else n_all - 1 - jj

    def spec(arr, dd):
        tail = arr.shape[3:]
        return pl.BlockSpec((None, None, None) + tail,
                            lambda bi, j: (bi, in_blk(dd, j), dd) + (0,) * len(tail))

    arrs = (lhs, cg, eg, qe, ca, er)
    in_specs = [spec(a, dd) for dd in range(N_DIR) for a in arrs]
    out_spec = lambda dd: pl.BlockSpec((None, CHUNK, O_W), lambda bi, j: (bi, out_blk(dd, j), 0))
    o_shape = jax.ShapeDtypeStruct((b, n_lat * CHUNK, O_W), F32)
    return pl.pallas_call(
        _scan_kernel,
        grid=(b, n_all),
        in_specs=in_specs,
        out_specs=[out_spec(0), out_spec(1)],
        out_shape=[o_shape, o_shape],
        scratch_shapes=[
            pltpu.VMEM((N_DIR, GLA_HEADS, GLA_DK, GLA_DV), F32),
            pltpu.VMEM((N_DIR, GDN_HEADS, GDN_DK, GDN_DV), F32),
        ],
        compiler_params=pltpu.CompilerParams(
            dimension_semantics=("arbitrary", "arbitrary"), vmem_limit_bytes=V7X_VMEM_LIMIT),
        name="state_scan",
    )(*(arrs + arrs))


def _merge_kernel(oi_ref, of_ref, ob_ref, zg_ref, x_ref, mod_ref, wa_ref, wb_ref, wo_ref, ga_ref, gb_ref,
                  bg_ref, fg_ref, y_ref, *, d_model):
    o = oi_ref[...] + of_ref[...] + ob_ref[...]
    zg = zg_ref[...]

    def head_norm(xh, gain):
        ms = jnp.mean(xh * xh, axis=-1, keepdims=True)
        return xh * lax.rsqrt(ms + EPS) * gain

    ya = jnp.concatenate(
        [head_norm(o[:, h * GLA_DV:(h + 1) * GLA_DV], ga_ref[...]) for h in range(GLA_HEADS)], axis=1)
    yb = jnp.concatenate(
        [head_norm(o[:, GLA_V + h * GDN_DV:GLA_V + (h + 1) * GDN_DV], gb_ref[...]) for h in range(GDN_HEADS)],
        axis=1)
    ya = ya * _silu(zg[:, :GLA_V])
    yb = yb * _silu(zg[:, GLA_V:GLA_V + GDN_V])
    pa = _dg(ya.astype(BF16), wa_ref[...])
    pb = _dg(yb.astype(BF16), wb_ref[...])
    gates = _sigmoid(zg[:, ZG_W:] + bg_ref[...])
    merged = gates[:, :d_model] * pa + gates[:, d_model:] * pb
    out = _dg(merged.astype(BF16), wo_ref[...])
    gate = mod_ref[...][:, 2 * d_model:]
    xn = x_ref[...] + gate * out
    ms = jnp.mean(xn * xn, axis=-1, keepdims=True)
    y_ref[...] = xn * lax.rsqrt(ms + EPS) * fg_ref[...]


def _merge_call(oi, of, ob, zg, x, mod3, wa, wb, wo, ga, gb, bg, fg):
    b, seq, d = x.shape
    tm = 256
    kern = functools.partial(_merge_kernel, d_model=d)
    full = lambda a: pl.BlockSpec(a.shape, lambda bi, t: (0,) * a.ndim)
    rows = lambda width: pl.BlockSpec((None, tm, width), lambda bi, t: (bi, t, 0))
    return pl.pallas_call(
        kern,
        grid=(b, seq // tm),
        in_specs=[
            rows(O_W), rows(O_W), rows(O_W), rows(zg.shape[2]), rows(d),
            pl.BlockSpec((None, 1, 3 * d), lambda bi, t: (bi, 0, 0)),
            full(wa), full(wb), full(wo), full(ga), full(gb), full(bg), full(fg),
        ],
        out_specs=rows(d),
        out_shape=jax.ShapeDtypeStruct((b, seq, d), x.dtype),
        compiler_params=pltpu.CompilerParams(
            dimension_semantics=("arbitrary", "arbitrary"), vmem_limit_bytes=V7X_VMEM_LIMIT),
        name="merge_out",
    )(oi, of, ob, zg, x, mod3, wa, wb, wo, ga, gb, bg, fg)


def _regroup_w_in(w):
    d = w.shape[0]
    o = 0
    offs = []
    for size in (GLA_QK, GLA_QK, GLA_V, GLA_V, N_DIR * GLA_RANK, GDN_QK, GDN_QK, GDN_V, GDN_V,
                 N_DIR * GDN_HEADS, N_DIR * GDN_HEADS, 2 * d):
        offs.append((o, o + size))
        o += size
    aq, ak, av, ag, alr, bq, bk, bv, bg, bbeta, bdec, mg = [w[:, s:e] for s, e in offs]
    small = jnp.concatenate([alr, bbeta, bdec], axis=1)
    small = jnp.pad(small, ((0, 0), (0, SMALL_W - small.shape[1])))
    return jnp.concatenate([aq, ak, av, bq, bk, bv, ag, bg, mg, small], axis=1).astype(BF16)


def kernel(x, c, ctx, c_ctx, w_mod, b_mod, norm_g, w_in, gla_up, gla_ub, gla_onorm, gdn_conv, gdn_a_log,
           gdn_dt_bias, gdn_onorm, w_gla_out, w_gdn_out, b_gate, w_o, final_g):
    b, seq, d = x.shape
    assert w_mod.shape[0] == 1, "single-layer block"
    assert seq % CHUNK == 0 and ctx.shape[1] % CHUNK == 0 and seq % ctx.shape[1] == 0
    n_lat = seq // CHUNK

    rows = -(-(b + 1) // 8) * 8
    cc = jnp.zeros((rows, d), F32).at[:b].set(c).at[b].set(c_ctx)
    mod = _mod_call(cc, w_mod[0], b_mod[0][None, :])
    mod3 = mod[:, None, :]

    w_r = _regroup_w_in(w_in[0])
    za, zb, zg, zs = _proj_call(x, ctx, mod3, norm_g[0][None, :], w_r)

    cw = gdn_conv[0].reshape(9, CONV_CH)
    up_bd = jnp.zeros((SMALL_W, N_DIR * GLA_QK), F32)
    for n in range(N_DIR):
        up_bd = up_bd.at[n * GLA_RANK:(n + 1) * GLA_RANK, n * GLA_QK:(n + 1) * GLA_QK].set(gla_up[0, n])
    ub = gla_ub[0].reshape(1, N_DIR * GLA_QK)
    nd = N_DIR * GDN_HEADS
    alv = jnp.zeros((1, SMALL_W), F32).at[0, LANE_DEC:LANE_DEC + nd].set(gdn_a_log[0].reshape(nd))
    dtbv = jnp.zeros((1, SMALL_W), F32).at[0, LANE_DEC:LANE_DEC + nd].set(gdn_dt_bias[0].reshape(nd))
    lhs, cg, eg, qe, ca, er, oi = _pre_call(zb, za, zs, cw, up_bd, ub, alv, dtbv, n_lat)

    of, ob = _scan_call(lhs, cg, eg, qe, ca, er, n_lat)

    return _merge_call(oi, of, ob, zg, x, mod3, w_gla_out[0].astype(BF16), w_gdn_out[0].astype(BF16),
                       w_o[0].astype(BF16), gla_onorm[0][None, :], gdn_onorm[0][None, :],
                       b_gate[0][None, :], final_g[None, :])
```

```python
import functools

import jax
import jax.numpy as jnp
from jax import lax
from jax.experimental import pallas as pl
from jax.experimental.pallas import tpu as pltpu

F32 = jnp.float32
BF16 = jnp.bfloat16

CHUNK = 64
N_DIR = 2
GLA_HEADS, GLA_DK, GLA_DV, GLA_RANK, GLA_TAU = 4, 64, 128, 16, 16.0
GLA_QK, GLA_V = GLA_HEADS * GLA_DK, GLA_HEADS * GLA_DV
GDN_HEADS, GDN_DK, GDN_DV = 4, 128, 128
GDN_QK, GDN_V = GDN_HEADS * GDN_DK, GDN_HEADS * GDN_DV
CONV_CH = 2 * GDN_QK + GDN_V
EPS = 1e-6
O_W = GLA_V + GDN_V

LANE_ALR = 0
LANE_BETA = N_DIR * GLA_RANK
LANE_DEC = LANE_BETA + N_DIR * GDN_HEADS
SMALL_W = 128

ZA_W = 2 * GLA_QK + GLA_V
ZB_W = CONV_CH
ZG_W = GLA_V + GDN_V

V7X_VMEM_LIMIT = 56 * 1024 * 1024

_NN = (((1,), (0,)), ((), ()))
_NT = (((1,), (1,)), ((), ()))
_TN = (((0,), (0,)), ((), ()))


def _dg(a, b, dims=_NN):
    return lax.dot_general(a, b, dims, preferred_element_type=F32)


def _dot3(a, b, dims=_NN):
    ah = a.astype(BF16)
    al = (a - ah.astype(F32)).astype(BF16)
    bh = b.astype(BF16)
    bl = (b - bh.astype(F32)).astype(BF16)
    return _dg(ah, bh, dims) + _dg(ah, bl, dims) + _dg(al, bh, dims)


def _cum(mask_bf, x, passes):
    acc = None
    rem = x
    for _ in range(passes):
        piece = rem.astype(BF16)
        term = _dg(mask_bf, piece)
        acc = term if acc is None else acc + term
        rem = rem - piece.astype(F32)
    return acc


def _sigmoid(x):
    return 1.0 / (1.0 + jnp.exp(-x))


def _silu(x):
    return x * _sigmoid(x)


def _softplus(x):
    return jnp.maximum(x, 0.0) + jnp.log1p(jnp.exp(-jnp.abs(x)))


def _tile_chunks(n_lat, n_ctx):
    for ch in (4, 2, 1):
        if n_lat % ch == 0 and n_ctx % ch == 0:
            return ch


def _mod_kernel(c_ref, w_ref, b_ref, o_ref):
    o_ref[...] = _dot3(_silu(c_ref[...]), w_ref[...]) + b_ref[...]


def _mod_call(cc, w_mod, b_mod):
    rows, d = cc.shape
    n = w_mod.shape[1]
    bn = d
    return pl.pallas_call(
        _mod_kernel,
        grid=(n // bn,),
        in_specs=[
            pl.BlockSpec((rows, d), lambda i: (0, 0)),
            pl.BlockSpec((d, bn), lambda i: (0, i)),
            pl.BlockSpec((1, bn), lambda i: (0, i)),
        ],
        out_specs=pl.BlockSpec((rows, bn), lambda i: (0, i)),
        out_shape=jax.ShapeDtypeStruct((rows, n), F32),
        compiler_params=pltpu.CompilerParams(dimension_semantics=("arbitrary",), vmem_limit_bytes=V7X_VMEM_LIMIT),
        name="adaln_mod",
    )(cc, w_mod, b_mod)


def _proj_kernel(x_ref, ctx_ref, mod_ref, g_ref, w_ref, za_ref, zb_ref, zg_ref, zs_ref, *, d_model):
    t = pl.program_id(1)
    xin = jnp.where(t == 0, ctx_ref[...], x_ref[...])
    ms = jnp.mean(xin * xin, axis=-1, keepdims=True)
    y = xin * lax.rsqrt(ms + EPS) * g_ref[...]
    mod = mod_ref[...]
    shift = mod[:, :d_model]
    scale = mod[:, d_model:2 * d_model]
    h = (y * (1.0 + scale) + shift).astype(BF16)
    o0, o1, o2, o3 = 0, ZA_W, ZA_W + ZB_W, ZA_W + ZB_W + ZG_W + 2 * d_model
    za_ref[...] = _dg(h, w_ref[:, o0:o1])
    zb_ref[...] = _dg(h, w_ref[:, o1:o2])
    zs_ref[...] = _dg(h, w_ref[:, o3:o3 + SMALL_W])

    @pl.when(t > 0)
    def _():
        zg_ref[...] = _dg(h, w_ref[:, o2:o3])


def _proj_call(x, ctx, mod3, norm_g, w_r):
    b, seq, d = x.shape
    ctx_len = ctx.shape[1]
    tm = ctx_len
    n_lat = seq // tm
    tot = seq + ctx_len
    zgw = ZG_W + 2 * d
    n_all = w_r.shape[1]

    def row_blk(bi, t):
        return jnp.where(t == 0, n_lat, t - 1)

    kern = functools.partial(_proj_kernel, d_model=d)
    return pl.pallas_call(
        kern,
        grid=(b, n_lat + 1),
        in_specs=[
            pl.BlockSpec((None, tm, d), lambda bi, t: (bi, jnp.maximum(t - 1, 0), 0)),
            pl.BlockSpec((None, tm, d), lambda bi, t: (bi, 0, 0)),
            pl.BlockSpec((None, 1, 3 * d), lambda bi, t: (jnp.where(t == 0, b, bi), 0, 0)),
            pl.BlockSpec((1, d), lambda bi, t: (0, 0)),
            pl.BlockSpec((d, n_all), lambda bi, t: (0, 0), pipeline_mode=pl.Buffered(1)),
        ],
        out_specs=[
            pl.BlockSpec((None, tm, ZA_W), lambda bi, t: (bi, row_blk(bi, t), 0)),
            pl.BlockSpec((None, tm, ZB_W), lambda bi, t: (bi, row_blk(bi, t), 0)),
            pl.BlockSpec((None, tm, zgw), lambda bi, t: (bi, jnp.maximum(t - 1, 0), 0)),
            pl.BlockSpec((None, tm, SMALL_W), lambda bi, t: (bi, row_blk(bi, t), 0)),
        ],
        out_shape=[
            jax.ShapeDtypeStruct((b, tot, ZA_W), F32),
            jax.ShapeDtypeStruct((b, tot, ZB_W), F32),
            jax.ShapeDtypeStruct((b, seq, zgw), F32),
            jax.ShapeDtypeStruct((b, tot, SMALL_W), F32),
        ],
        compiler_params=pltpu.CompilerParams(
            dimension_semantics=("arbitrary", "arbitrary"), vmem_limit_bytes=V7X_VMEM_LIMIT),
        name="in_proj",
    )(x, ctx, mod3, norm_g, w_r)


def _tri_inverse_many(a_list, eye_f):
    ns = [-a for a in a_list]
    ps = [eye_f + n for n in ns]
    steps = max(1, (CHUNK - 1).bit_length()) - 1
    for _ in range(steps):
        nbs = [n.astype(BF16) for n in ns]
        ns = [_dg(nb, nb) for nb in nbs]
        nbs = [n.astype(BF16) for n in ns]
        ps = [p + _dg(p.astype(BF16), nb) for p, nb in zip(ps, nbs)]
    return ps


def _pre_kernel(zp_ref, zc_ref, zn_ref, za_ref, zs_ref, cw_ref, up_ref, ub_ref, al_ref, dtb_ref,
                lhs_ref, cg_ref, eg_ref, qe_ref, ca_ref, er_ref, oi_ref, conv_ref, *, nt_lat, nt_all, ch):
    t = pl.program_id(1)
    is_ctx = t >= nt_lat
    row = lax.broadcasted_iota(jnp.int32, (CHUNK, 1), 0)
    w = cw_ref[...]
    rows_of = lambda ci: slice(ci * CHUNK, (ci + 1) * CHUNK)

    @pl.when(jnp.logical_not(is_ctx))
    def _():
        pv = (t > 0).astype(F32)
        nv = (t < nt_lat - 1).astype(F32)
        for ci in range(ch):
            prev = zp_ref[...] * pv if ci == 0 else zc_ref[rows_of(ci - 1), :]
            nxt = zn_ref[...] * nv if ci == ch - 1 else zc_ref[rows_of(ci + 1), :]
            cur = zc_ref[rows_of(ci), :]

            def col(dc):
                return w[dc:dc + 1] * prev + w[3 + dc:4 + dc] * cur + w[6 + dc:7 + dc] * nxt

            left = jnp.where(row >= 1, pltpu.roll(col(0), 1, 0), 0.0)
            right = jnp.where(row <= CHUNK - 2, pltpu.roll(col(2), CHUNK - 1, 0), 0.0)
            conv_ref[rows_of(ci), :] = col(1) + left + right

    @pl.when(is_ctx)
    def _():
        pv = (t > nt_lat).astype(F32)
        nv = (t < nt_all - 1).astype(F32)
        for ci in range(ch):
            cur = zc_ref[rows_of(ci), :]
            before = (zp_ref[CHUNK - 1:CHUNK, :] * pv if ci == 0
                      else zc_ref[ci * CHUNK - 1:ci * CHUNK, :])
            after = (zn_ref[0:1, :] * nv if ci == ch - 1
                     else zc_ref[(ci + 1) * CHUNK:(ci + 1) * CHUNK + 1, :])
            xm = jnp.where(row == CHUNK - 1, before, cur)
            xp = jnp.where(row == 0, after, cur)
            conv_ref[rows_of(ci), :] = (w[4:5] * cur + pltpu.roll(w[3:4] * xm, 1, 0)
                                        + pltpu.roll(w[5:6] * xp, CHUNK - 1, 0))

    dirs = range(N_DIR)
    chunks = range(ch)
    r = lax.broadcasted_iota(jnp.int32, (CHUNK, CHUNK), 0)
    cc = lax.broadcasted_iota(jnp.int32, (CHUNK, CHUNK), 1)
    incl = [cc <= r, cc >= r]
    strict = [cc < r, cc > r]
    eye_f = (cc == r).astype(F32)
    cum_bf = [m.astype(BF16) for m in incl]
    last = [lambda x: x[CHUNK - 1:CHUNK], lambda x: x[0:1]]
    lane = lax.broadcasted_iota(jnp.int32, (1, SMALL_W), 1)

    qs, ks, vs, g_all, beta_all, glog_all = [], [], [], [], [], []
    for ci in chunks:
        a = _silu(conv_ref[rows_of(ci), :])
        qc, kc = [], []
        for hh in range(2 * GDN_HEADS):
            xh = a[:, hh * GDN_DK:(hh + 1) * GDN_DK]
            inv = lax.rsqrt(jnp.sum(xh * xh, axis=-1, keepdims=True) + EPS)
            if hh < GDN_HEADS:
                qc.append(xh * (inv * GDN_DK ** -0.5))
            else:
                kc.append(xh * inv)
        qs.append(qc)
        ks.append(kc)
        vs.append([a[:, 2 * GDN_QK + h * GDN_DV:2 * GDN_QK + (h + 1) * GDN_DV] for h in range(GDN_HEADS)])
        zs = zs_ref[rows_of(ci), :]
        logits = _dot3(zs, up_ref[...]) + ub_ref[...]
        g_all.append((jnp.minimum(logits, 0.0) - jnp.log1p(jnp.exp(-jnp.abs(logits)))) * (1.0 / GLA_TAU))
        beta_all.append(_sigmoid(zs))
        glog_all.append(-jnp.exp(al_ref[...]) * _softplus(zs + dtb_ref[...]))

    sk = lambda h: slice(h * GLA_DK, (h + 1) * GLA_DK)
    cd = [(ci, d) for ci in chunks for d in dirs]
    v_bf, qe, kd, kl, bl = {}, {}, {}, {}, {}
    for ci in chunks:
        za = za_ref[rows_of(ci), :]
        q = za[:, :GLA_QK] * (GLA_DK ** -0.5)
        k = za[:, GLA_QK:2 * GLA_QK]
        v_bf[ci] = [za[:, 2 * GLA_QK + h * GLA_DV:2 * GLA_QK + (h + 1) * GLA_DV].astype(BF16)
                    for h in range(GLA_HEADS)]
        for d in dirs:
            bcum = _cum(cum_bf[d], g_all[ci][:, d * GLA_QK:(d + 1) * GLA_QK], 2)
            bl[ci, d] = last[d](bcum)
            qe[ci, d] = (q * jnp.exp(bcum)).astype(BF16)
            kd[ci, d] = (k * jnp.exp(-bcum)).astype(BF16)
            kl[ci, d] = (k * jnp.exp(bl[ci, d] - bcum)).astype(BF16)
    hd_a = [(ci, d, h) for ci, d in cd for h in range(GLA_HEADS)]
    att = [jnp.where(incl[d], _dg(qe[ci, d][:, sk(h)], kd[ci, d][:, sk(h)], _NT), 0.0).astype(BF16)
           for ci, d, h in hd_a]
    oi_a = {key: _dg(att[i], v_bf[key[0]][key[2]]) for i, key in enumerate(hd_a)}
    c_a = [_dg(kl[ci, d][:, sk(h)], v_bf[ci][h], _TN) for ci, d, h in hd_a]
    for ci, d in cd:
        qe_ref[ci, d] = qe[ci, d]
        er_ref[ci, d] = jnp.exp(bl[ci, d])
    for i, (ci, d, h) in enumerate(hd_a):
        ca_ref[ci, d, :, h * GLA_DV:(h + 1) * GLA_DV] = c_a[i].astype(BF16)

    gam_all = {key: _cum(cum_bf[key[1]], glog_all[key[0]], 3) for key in cd}
    gam_t = {key: gam_all[key].T for key in cd}
    ch_h = [(ci, h) for ci in chunks for h in range(GDN_HEADS)]
    k_bf = {(ci, h): ks[ci][h].astype(BF16) for ci, h in ch_h}
    raw = {(ci, h): _dg(jnp.concatenate([ks[ci][h], qs[ci][h]], axis=0).astype(BF16), k_bf[ci, h], _NT)
           for ci, h in ch_h}
    hd_b = [(ci, d, h) for ci, d in cd for h in range(GDN_HEADS)]
    gam_c, beta_c, gl, dm = [], [], [], []
    for ci, d, h in hd_b:
        ld, lb = LANE_DEC + d * GDN_HEADS + h, LANE_BETA + d * GDN_HEADS + h
        gc = gam_all[ci, d][:, ld:ld + 1]
        gr = gam_t[ci, d][ld:ld + 1, :]
        gam_c.append(gc)
        beta_c.append(beta_all[ci][:, lb:lb + 1])
        gl.append(last[d](gc))
        dm.append(jnp.where(incl[d], jnp.exp(jnp.minimum(gc - gr, 0.0)), 0.0))
    n_b = len(hd_b)
    a_low = [jnp.where(strict[d], raw[ci, h][:CHUNK] * beta_c[i] * dm[i], 0.0) for i, (ci, d, h) in enumerate(hd_b)]
    aqk = [(raw[ci, h][CHUNK:] * dm[i]).astype(BF16) for i, (ci, d, h) in enumerate(hd_b)]
    tmat = _tri_inverse_many(a_low, eye_f)
    egc = [jnp.exp(gc) for gc in gam_c]
    rhs = [jnp.concatenate([vs[ci][h] * beta_c[i], ks[ci][h] * (beta_c[i] * egc[i])], axis=1).astype(BF16)
           for i, (ci, d, h) in enumerate(hd_b)]
    uw = [_dg(tmat[i].astype(BF16), rhs[i]).astype(BF16) for i in range(n_b)]
    aw = [_dg(aqk[i], uw[i]) for i in range(n_b)]
    khat = [(ks[ci][h] * jnp.exp(gl[i] - gam_c[i])).astype(BF16) for i, (ci, d, h) in enumerate(hd_b)]
    pc = [_dg(khat[i], uw[i], _TN) for i in range(n_b)]
    for ci, d in cd:
        row_e = jnp.zeros((1, SMALL_W), F32)
        for h in range(GDN_HEADS):
            row_e = jnp.where(lane == h, jnp.exp(gl[hd_b.index((ci, d, h))]), row_e)
        eg_ref[ci, d] = row_e
    for i, (ci, d, h) in enumerate(hd_b):
        lhs_ref[ci, d, h, :GDN_DK] = pc[i][:, GDN_DV:].astype(BF16)
        lhs_ref[ci, d, h, GDN_DK:] = (qs[ci][h] * egc[i] - aw[i][:, GDN_DV:]).astype(BF16)
        cg_ref[ci, d, h] = pc[i][:, :GDN_DV].astype(BF16)

    @pl.when(jnp.logical_not(is_ctx))
    def _():
        for ci in chunks:
            for h in range(GLA_HEADS):
                oi_ref[rows_of(ci), h * GLA_DV:(h + 1) * GLA_DV] = oi_a[ci, 0, h] + oi_a[ci, 1, h]
            for h in range(GDN_HEADS):
                i0, i1 = hd_b.index((ci, 0, h)), hd_b.index((ci, 1, h))
                oi_ref[rows_of(ci), GLA_V + h * GDN_DV:GLA_V + (h + 1) * GDN_DV] = (
                    aw[i0][:, :GDN_DV] + aw[i1][:, :GDN_DV])


def _pre_call(zb, za, zs, cw, up_bd, ub, alv, dtbv, n_lat, ch):
    b, tot, _ = zb.shape
    n_all = tot // CHUNK
    nt_lat, nt_all = n_lat // ch, n_all // ch
    tr = ch * CHUNK

    def prev_blk(bi, t):
        return jnp.where(t >= nt_lat, jnp.maximum(t * ch - 1, n_lat), jnp.maximum(t * ch - 1, 0))

    def next_blk(bi, t):
        return jnp.where(t >= nt_lat, jnp.minimum((t + 1) * ch, n_all - 1), jnp.minimum((t + 1) * ch, n_lat - 1))

    kern = functools.partial(_pre_kernel, nt_lat=nt_lat, nt_all=nt_all, ch=ch)
    full = lambda shape: pl.BlockSpec(shape, lambda bi, t: (0,) * len(shape))
    per_tile = lambda *tail: pl.BlockSpec((None, ch) + tail, lambda bi, t: (bi, t) + (0,) * len(tail))
    return pl.pallas_call(
        kern,
        grid=(b, nt_all),
        in_specs=[
            pl.BlockSpec((None, CHUNK, ZB_W), lambda bi, t: (bi, prev_blk(bi, t), 0)),
            pl.BlockSpec((None, tr, ZB_W), lambda bi, t: (bi, t, 0)),
            pl.BlockSpec((None, CHUNK, ZB_W), lambda bi, t: (bi, next_blk(bi, t), 0)),
            pl.BlockSpec((None, tr, ZA_W), lambda bi, t: (bi, t, 0)),
            pl.BlockSpec((None, tr, SMALL_W), lambda bi, t: (bi, t, 0)),
            full(cw.shape), full(up_bd.shape), full(ub.shape), full(alv.shape), full(dtbv.shape),
        ],
        out_specs=[
            per_tile(N_DIR, GDN_HEADS, GDN_DK + CHUNK, GDN_DV),
            per_tile(N_DIR, GDN_HEADS, GDN_DK, GDN_DV),
            per_tile(N_DIR, 1, SMALL_W),
            per_tile(N_DIR, CHUNK, GLA_QK),
            per_tile(N_DIR, CHUNK, GLA_V),
            per_tile(N_DIR, 1, GLA_QK),
            pl.BlockSpec((None, tr, O_W), lambda bi, t: (bi, jnp.minimum(t, nt_lat - 1), 0)),
        ],
        out_shape=[
            jax.ShapeDtypeStruct((b, n_all, N_DIR, GDN_HEADS, GDN_DK + CHUNK, GDN_DV), BF16),
            jax.ShapeDtypeStruct((b, n_all, N_DIR, GDN_HEADS, GDN_DK, GDN_DV), BF16),
            jax.ShapeDtypeStruct((b, n_all, N_DIR, 1, SMALL_W), F32),
            jax.ShapeDtypeStruct((b, n_all, N_DIR, CHUNK, GLA_QK), BF16),
            jax.ShapeDtypeStruct((b, n_all, N_DIR, CHUNK, GLA_V), BF16),
            jax.ShapeDtypeStruct((b, n_all, N_DIR, 1, GLA_QK), F32),
            jax.ShapeDtypeStruct((b, n_lat * CHUNK, O_W), F32),
        ],
        scratch_shapes=[pltpu.VMEM((tr, ZB_W), F32)],
        compiler_params=pltpu.CompilerParams(
            dimension_semantics=("arbitrary", "arbitrary"), vmem_limit_bytes=V7X_VMEM_LIMIT),
        name="chunk_pre",
    )(zb, zb, zb, za, zs, cw, up_bd, ub, alv, dtbv)


def _scan_kernel(*refs, ch):
    ins, (of_ref, ob_ref, sa_ref, sb_ref) = refs[:12], refs[12:]
    per_dir = [ins[0:6], ins[6:12]]
    o_refs = [of_ref, ob_ref]
    j = pl.program_id(1)

    @pl.when(j == 0)
    def _():
        sa_ref[...] = jnp.zeros_like(sa_ref)
        sb_ref[...] = jnp.zeros_like(sb_ref)

    r = lax.broadcasted_iota(jnp.int32, (GLA_DK, GLA_DK), 0)
    cc = lax.broadcasted_iota(jnp.int32, (GLA_DK, GLA_DK), 1)
    eye = cc == r

    hd = [(d, h) for d in range(N_DIR) for h in range(GDN_HEADS)]
    s_b = [sb_ref[d, h] for d, h in hd]
    s_a = [sa_ref[d, h] for d, h in hd]
    for s in range(ch):
        ci = [s, ch - 1 - s]
        rows = [slice(c * CHUNK, (c + 1) * CHUNK) for c in ci]
        res_b = [_dg(per_dir[d][0][ci[d], h], s_b[i].astype(BF16)) for i, (d, h) in enumerate(hd)]
        for i, (d, h) in enumerate(hd):
            e = per_dir[d][2][ci[d], :, h:h + 1]
            s_b[i] = s_b[i] * e - res_b[i][:GDN_DK] + per_dir[d][1][ci[d], h].astype(F32)
            o_refs[d][rows[d], GLA_V + h * GDN_DV:GLA_V + (h + 1) * GDN_DV] = res_b[i][GDN_DK:]
        res_a = [_dg(per_dir[d][3][ci[d], :, h * GLA_DK:(h + 1) * GLA_DK], s_a[i].astype(BF16))
                 for i, (d, h) in enumerate(hd)]
        for i, (d, h) in enumerate(hd):
            e_row = per_dir[d][5][ci[d], :, h * GLA_DK:(h + 1) * GLA_DK]
            ecol = jnp.sum(jnp.where(eye, e_row, 0.0), axis=1, keepdims=True)
            s_a[i] = s_a[i] * ecol + per_dir[d][4][ci[d], :, h * GLA_DV:(h + 1) * GLA_DV].astype(F32)
            o_refs[d][rows[d], h * GLA_DV:(h + 1) * GLA_DV] = res_a[i]
    for i, (d, h) in enumerate(hd):
        sb_ref[d, h] = s_b[i]
        sa_ref[d, h] = s_a[i]


def _scan_call(lhs, cg, eg, qe, ca, er, n_lat, ch):
    b, n_all = lhs.shape[0], lhs.shape[1]
    nt_all, nt_lat = n_all // ch, n_lat // ch
    nt_ctx = nt_all - nt_lat
    tr = ch * CHUNK

    def in_blk(dd, j):
        if dd == 0:
            return jnp.where(j < nt_ctx, nt_lat + j, j - nt_ctx)
        return nt_all - 1 - j

    def out_blk(dd, j):
        jj = jnp.maximum(j, nt_ctx)
        return jj - nt_ctx if dd == 0 else nt_all - 1 - jj

    def spec(arr, dd):
        tail = arr.shape[3:]
        return pl.BlockSpec((None, ch, None) + tail,
                            lambda bi, j: (bi, in_blk(dd, j), dd) + (0,) * len(tail))

    arrs = (lhs, cg, eg, qe, ca, er)
    in_specs = [spec(a, dd) for dd in range(N_DIR) for a in arrs]
    out_spec = lambda dd: pl.BlockSpec((None, tr, O_W), lambda bi, j: (bi, out_blk(dd, j), 0))
    o_shape = jax.ShapeDtypeStruct((b, n_lat * CHUNK, O_W), F32)
    return pl.pallas_call(
        functools.partial(_scan_kernel, ch=ch),
        grid=(b, nt_all),
        in_specs=in_specs,
        out_specs=[out_spec(0), out_spec(1)],
        out_shape=[o_shape, o_shape],
        scratch_shapes=[
            pltpu.VMEM((N_DIR, GLA_HEADS, GLA_DK, GLA_DV), F32),
            pltpu.VMEM((N_DIR, GDN_HEADS, GDN_DK, GDN_DV), F32),
        ],
        compiler_params=pltpu.CompilerParams(
            dimension_semantics=("arbitrary", "arbitrary"), vmem_limit_bytes=V7X_VMEM_LIMIT),
        name="state_scan",
    )(*(arrs + arrs))


def _merge_kernel(oi_ref, of_ref, ob_ref, zg_ref, x_ref, mod_ref, wa_ref, wb_ref, wo_ref, ga_ref, gb_ref,
                  bg_ref, fg_ref, y_ref, *, d_model):
    o = oi_ref[...] + of_ref[...] + ob_ref[...]
    zg = zg_ref[...]

    def head_norm(xh, gain):
        ms = jnp.mean(xh * xh, axis=-1, keepdims=True)
        return xh * lax.rsqrt(ms + EPS) * gain

    ya = jnp.concatenate(
        [head_norm(o[:, h * GLA_DV:(h + 1) * GLA_DV], ga_ref[...]) for h in range(GLA_HEADS)], axis=1)
    yb = jnp.concatenate(
        [head_norm(o[:, GLA_V + h * GDN_DV:GLA_V + (h + 1) * GDN_DV], gb_ref[...]) for h in range(GDN_HEADS)],
        axis=1)
    ya = ya * _silu(zg[:, :GLA_V])
    yb = yb * _silu(zg[:, GLA_V:GLA_V + GDN_V])
    pa = _dg(ya.astype(BF16), wa_ref[...])
    pb = _dg(yb.astype(BF16), wb_ref[...])
    gates = _sigmoid(zg[:, ZG_W:] + bg_ref[...])
    merged = gates[:, :d_model] * pa + gates[:, d_model:] * pb
    out = _dg(merged.astype(BF16), wo_ref[...])
    gate = mod_ref[...][:, 2 * d_model:]
    xn = x_ref[...] + gate * out
    ms = jnp.mean(xn * xn, axis=-1, keepdims=True)
    y_ref[...] = xn * lax.rsqrt(ms + EPS) * fg_ref[...]


def _merge_call(oi, of, ob, zg, x, mod3, wa, wb, wo, ga, gb, bg, fg):
    b, seq, d = x.shape
    tm = 256
    kern = functools.partial(_merge_kernel, d_model=d)
    full = lambda a: pl.BlockSpec(a.shape, lambda bi, t: (0,) * a.ndim)
    rows = lambda width: pl.BlockSpec((None, tm, width), lambda bi, t: (bi, t, 0))
    return pl.pallas_call(
        kern,
        grid=(b, seq // tm),
        in_specs=[
            rows(O_W), rows(O_W), rows(O_W), rows(zg.shape[2]), rows(d),
            pl.BlockSpec((None, 1, 3 * d), lambda bi, t: (bi, 0, 0)),
            full(wa), full(wb), full(wo), full(ga), full(gb), full(bg), full(fg),
        ],
        out_specs=rows(d),
        out_shape=jax.ShapeDtypeStruct((b, seq, d), x.dtype),
        compiler_params=pltpu.CompilerParams(
            dimension_semantics=("arbitrary", "arbitrary"), vmem_limit_bytes=V7X_VMEM_LIMIT),
        name="merge_out",
    )(oi, of, ob, zg, x, mod3, wa, wb, wo, ga, gb, bg, fg)


def _regroup_w_in(w):
    d = w.shape[0]
    o = 0
    offs = []
    for size in (GLA_QK, GLA_QK, GLA_V, GLA_V, N_DIR * GLA_RANK, GDN_QK, GDN_QK, GDN_V, GDN_V,
                 N_DIR * GDN_HEADS, N_DIR * GDN_HEADS, 2 * d):
        offs.append((o, o + size))
        o += size
    aq, ak, av, ag, alr, bq, bk, bv, bg, bbeta, bdec, mg = [w[:, s:e] for s, e in offs]
    small = jnp.concatenate([alr, bbeta, bdec], axis=1)
    small = jnp.pad(small, ((0, 0), (0, SMALL_W - small.shape[1])))
    return jnp.concatenate([aq, ak, av, bq, bk, bv, ag, bg, mg, small], axis=1).astype(BF16)


def kernel(x, c, ctx, c_ctx, w_mod, b_mod, norm_g, w_in, gla_up, gla_ub, gla_onorm, gdn_conv, gdn_a_log,
           gdn_dt_bias, gdn_onorm, w_gla_out, w_gdn_out, b_gate, w_o, final_g):
    b, seq, d = x.shape
    assert w_mod.shape[0] == 1, "single-layer block"
    assert seq % CHUNK == 0 and ctx.shape[1] % CHUNK == 0 and seq % ctx.shape[1] == 0
    n_lat = seq // CHUNK
    ch = _tile_chunks(n_lat, ctx.shape[1] // CHUNK)

    rows = -(-(b + 1) // 8) * 8
    cc = jnp.zeros((rows, d), F32).at[:b].set(c).at[b].set(c_ctx)
    mod = _mod_call(cc, w_mod[0], b_mod[0][None, :])
    mod3 = mod[:, None, :]

    w_r = _regroup_w_in(w_in[0])
    za, zb, zg, zs = _proj_call(x, ctx, mod3, norm_g[0][None, :], w_r)

    cw = gdn_conv[0].reshape(9, CONV_CH)
    up_bd = jnp.zeros((SMALL_W, N_DIR * GLA_QK), F32)
    for n in range(N_DIR):
        up_bd = up_bd.at[n * GLA_RANK:(n + 1) * GLA_RANK, n * GLA_QK:(n + 1) * GLA_QK].set(gla_up[0, n])
    ub = gla_ub[0].reshape(1, N_DIR * GLA_QK)
    nd = N_DIR * GDN_HEADS
    alv = jnp.zeros((1, SMALL_W), F32).at[0, LANE_DEC:LANE_DEC + nd].set(gdn_a_log[0].reshape(nd))
    dtbv = jnp.zeros((1, SMALL_W), F32).at[0, LANE_DEC:LANE_DEC + nd].set(gdn_dt_bias[0].reshape(nd))
    lhs, cg, eg, qe, ca, er, oi = _pre_call(zb, za, zs, cw, up_bd, ub, alv, dtbv, n_lat, ch)

    of, ob = _scan_call(lhs, cg, eg, qe, ca, er, n_lat, ch)

    return _merge_call(oi, of, ob, zg, x, mod3, w_gla_out[0].astype(BF16), w_gdn_out[0].astype(BF16),
                       w_o[0].astype(BF16), gla_onorm[0][None, :], gdn_onorm[0][None, :],
                       b_gate[0][None, :], final_g[None, :])
```

```python
import functools

import jax
import jax.numpy as jnp
from jax import lax
from jax.experimental import pallas as pl
from jax.experimental.pallas import tpu as pltpu

F32 = jnp.float32
BF16 = jnp.bfloat16

CHUNK = 64
N_DIR = 2
GLA_HEADS, GLA_DK, GLA_DV, GLA_RANK, GLA_TAU = 4, 64, 128, 16, 16.0
GLA_QK, GLA_V = GLA_HEADS * GLA_DK, GLA_HEADS * GLA_DV
GDN_HEADS, GDN_DK, GDN_DV = 4, 128, 128
GDN_QK, GDN_V = GDN_HEADS * GDN_DK, GDN_HEADS * GDN_DV
CONV_CH = 2 * GDN_QK + GDN_V
EPS = 1e-6
O_W = GLA_V + GDN_V

LANE_ALR = 0
LANE_BETA = N_DIR * GLA_RANK
LANE_DEC = LANE_BETA + N_DIR * GDN_HEADS
SMALL_W = 128

ZA_W = 2 * GLA_QK + GLA_V
ZB_W = CONV_CH
ZG_W = GLA_V + GDN_V

V7X_VMEM_LIMIT = 56 * 1024 * 1024

_NN = (((1,), (0,)), ((), ()))
_NT = (((1,), (1,)), ((), ()))
_TN = (((0,), (0,)), ((), ()))


def _dg(a, b, dims=_NN):
    return lax.dot_general(a, b, dims, preferred_element_type=F32)


def _dot3(a, b, dims=_NN):
    ah = a.astype(BF16)
    al = (a - ah.astype(F32)).astype(BF16)
    bh = b.astype(BF16)
    bl = (b - bh.astype(F32)).astype(BF16)
    return _dg(ah, bh, dims) + _dg(ah, bl, dims) + _dg(al, bh, dims)


def _cum(mask_bf, x, passes):
    acc = None
    rem = x
    for _ in range(passes):
        piece = rem.astype(BF16)
        term = _dg(mask_bf, piece)
        acc = term if acc is None else acc + term
        rem = rem - piece.astype(F32)
    return acc


def _sigmoid(x):
    return 1.0 / (1.0 + jnp.exp(-x))


def _silu(x):
    return x * _sigmoid(x)


def _softplus(x):
    return jnp.maximum(x, 0.0) + jnp.log1p(jnp.exp(-jnp.abs(x)))


def _tile_chunks(n_lat, n_ctx):
    for ch in (4, 2, 1):
        if n_lat % ch == 0 and n_ctx % ch == 0:
            return ch


def _mod_kernel(c_ref, w_ref, b_ref, o_ref):
    o_ref[...] = _dot3(_silu(c_ref[...]), w_ref[...]) + b_ref[...]


def _mod_call(cc, w_mod, b_mod):
    rows, d = cc.shape
    n = w_mod.shape[1]
    bn = d
    return pl.pallas_call(
        _mod_kernel,
        grid=(n // bn,),
        in_specs=[
            pl.BlockSpec((rows, d), lambda i: (0, 0)),
            pl.BlockSpec((d, bn), lambda i: (0, i)),
            pl.BlockSpec((1, bn), lambda i: (0, i)),
        ],
        out_specs=pl.BlockSpec((rows, bn), lambda i: (0, i)),
        out_shape=jax.ShapeDtypeStruct((rows, n), F32),
        compiler_params=pltpu.CompilerParams(dimension_semantics=("arbitrary",), vmem_limit_bytes=V7X_VMEM_LIMIT),
        name="adaln_mod",
    )(cc, w_mod, b_mod)


def _proj_kernel(x_ref, ctx_ref, mod_ref, g_ref, w_ref, za_ref, zb_ref, zg_ref, zs_ref, *, d_model):
    t = pl.program_id(1)
    xin = jnp.where(t == 0, ctx_ref[...], x_ref[...])
    ms = jnp.mean(xin * xin, axis=-1, keepdims=True)
    y = xin * lax.rsqrt(ms + EPS) * g_ref[...]
    mod = mod_ref[...]
    shift = mod[:, :d_model]
    scale = mod[:, d_model:2 * d_model]
    h = (y * (1.0 + scale) + shift).astype(BF16)
    o0, o1, o2, o3 = 0, ZA_W, ZA_W + ZB_W, ZA_W + ZB_W + ZG_W + 2 * d_model
    za_ref[...] = _dg(h, w_ref[:, o0:o1])
    zb_ref[...] = _dg(h, w_ref[:, o1:o2])
    zs_ref[...] = _dg(h, w_ref[:, o3:o3 + SMALL_W])

    @pl.when(t > 0)
    def _():
        zg_ref[...] = _dg(h, w_ref[:, o2:o3])


def _proj_call(x, ctx, mod3, norm_g, w_r):
    b, seq, d = x.shape
    ctx_len = ctx.shape[1]
    tm = ctx_len
    n_lat = seq // tm
    tot = seq + ctx_len
    zgw = ZG_W + 2 * d
    n_all = w_r.shape[1]

    def row_blk(bi, t):
        return jnp.where(t == 0, n_lat, t - 1)

    kern = functools.partial(_proj_kernel, d_model=d)
    return pl.pallas_call(
        kern,
        grid=(b, n_lat + 1),
        in_specs=[
            pl.BlockSpec((None, tm, d), lambda bi, t: (bi, jnp.maximum(t - 1, 0), 0)),
            pl.BlockSpec((None, tm, d), lambda bi, t: (bi, 0, 0)),
            pl.BlockSpec((None, 1, 3 * d), lambda bi, t: (jnp.where(t == 0, b, bi), 0, 0)),
            pl.BlockSpec((1, d), lambda bi, t: (0, 0)),
            pl.BlockSpec((d, n_all), lambda bi, t: (0, 0), pipeline_mode=pl.Buffered(1)),
        ],
        out_specs=[
            pl.BlockSpec((None, tm, ZA_W), lambda bi, t: (bi, row_blk(bi, t), 0)),
            pl.BlockSpec((None, tm, ZB_W), lambda bi, t: (bi, row_blk(bi, t), 0)),
            pl.BlockSpec((None, tm, zgw), lambda bi, t: (bi, jnp.maximum(t - 1, 0), 0)),
            pl.BlockSpec((None, tm, SMALL_W), lambda bi, t: (bi, row_blk(bi, t), 0)),
        ],
        out_shape=[
            jax.ShapeDtypeStruct((b, tot, ZA_W), F32),
            jax.ShapeDtypeStruct((b, tot, ZB_W), F32),
            jax.ShapeDtypeStruct((b, seq, zgw), F32),
            jax.ShapeDtypeStruct((b, tot, SMALL_W), F32),
        ],
        compiler_params=pltpu.CompilerParams(
            dimension_semantics=("arbitrary", "arbitrary"), vmem_limit_bytes=V7X_VMEM_LIMIT),
        name="in_proj",
    )(x, ctx, mod3, norm_g, w_r)


INV_BLOCK = 16


def _tri_inverse_many(a_list, eye_f, blk_f):
    ads = [a * blk_f for a in a_list]
    aos = [(a - ad).astype(BF16) for a, ad in zip(a_list, ads)]
    ns = [-ad for ad in ads]
    ds = [eye_f + n for n in ns]
    for _ in range(INV_BLOCK.bit_length() - 2):
        nbs = [n.astype(BF16) for n in ns]
        ns = [_dg(nb, nb) for nb in nbs]
        nbs = [n.astype(BF16) for n in ns]
        ds = [d + _dg(d.astype(BF16), nb) for d, nb in zip(ds, nbs)]
    dbs = [d.astype(BF16) for d in ds]
    ms = [-_dg(db, ao) for db, ao in zip(dbs, aos)]
    mbs = [m.astype(BF16) for m in ms]
    m2s = [_dg(mb, mb) for mb in mbs]
    m3s = [_dg(mb, m2.astype(BF16)) for mb, m2 in zip(mbs, m2s)]
    ss = [(eye_f + m + m2 + m3).astype(BF16) for m, m2, m3 in zip(ms, m2s, m3s)]
    return [_dg(s, db) for s, db in zip(ss, dbs)]


def _pre_kernel(zp_ref, zc_ref, zn_ref, za_ref, zs_ref, cw_ref, up_ref, ub_ref, al_ref, dtb_ref,
                lhs_ref, cg_ref, eg_ref, qe_ref, ca_ref, er_ref, oi_ref, conv_ref, *, nt_lat, nt_all, ch):
    t = pl.program_id(1)
    is_ctx = t >= nt_lat
    row = lax.broadcasted_iota(jnp.int32, (CHUNK, 1), 0)
    w = cw_ref[...]
    rows_of = lambda ci: slice(ci * CHUNK, (ci + 1) * CHUNK)

    @pl.when(jnp.logical_not(is_ctx))
    def _():
        pv = (t > 0).astype(F32)
        nv = (t < nt_lat - 1).astype(F32)
        for ci in range(ch):
            prev = zp_ref[...] * pv if ci == 0 else zc_ref[rows_of(ci - 1), :]
            nxt = zn_ref[...] * nv if ci == ch - 1 else zc_ref[rows_of(ci + 1), :]
            cur = zc_ref[rows_of(ci), :]

            def col(dc):
                return w[dc:dc + 1] * prev + w[3 + dc:4 + dc] * cur + w[6 + dc:7 + dc] * nxt

            left = jnp.where(row >= 1, pltpu.roll(col(0), 1, 0), 0.0)
            right = jnp.where(row <= CHUNK - 2, pltpu.roll(col(2), CHUNK - 1, 0), 0.0)
            conv_ref[rows_of(ci), :] = col(1) + left + right

    @pl.when(is_ctx)
    def _():
        pv = (t > nt_lat).astype(F32)
        nv = (t < nt_all - 1).astype(F32)
        for ci in range(ch):
            cur = zc_ref[rows_of(ci), :]
            before = (zp_ref[CHUNK - 1:CHUNK, :] * pv if ci == 0
                      else zc_ref[ci * CHUNK - 1:ci * CHUNK, :])
            after = (zn_ref[0:1, :] * nv if ci == ch - 1
                     else zc_ref[(ci + 1) * CHUNK:(ci + 1) * CHUNK + 1, :])
            xm = jnp.where(row == CHUNK - 1, before, cur)
            xp = jnp.where(row == 0, after, cur)
            conv_ref[rows_of(ci), :] = (w[4:5] * cur + pltpu.roll(w[3:4] * xm, 1, 0)
                                        + pltpu.roll(w[5:6] * xp, CHUNK - 1, 0))

    dirs = range(N_DIR)
    chunks = range(ch)
    r = lax.broadcasted_iota(jnp.int32, (CHUNK, CHUNK), 0)
    cc = lax.broadcasted_iota(jnp.int32, (CHUNK, CHUNK), 1)
    incl = [cc <= r, cc >= r]
    strict = [cc < r, cc > r]
    eye_f = (cc == r).astype(F32)
    cum_bf = [m.astype(BF16) for m in incl]
    last = [lambda x: x[CHUNK - 1:CHUNK], lambda x: x[0:1]]
    lane = lax.broadcasted_iota(jnp.int32, (1, SMALL_W), 1)

    qs, ks, vs, g_all, beta_all, glog_all = [], [], [], [], [], []
    for ci in chunks:
        a = _silu(conv_ref[rows_of(ci), :])
        qc, kc = [], []
        for hh in range(2 * GDN_HEADS):
            xh = a[:, hh * GDN_DK:(hh + 1) * GDN_DK]
            inv = lax.rsqrt(jnp.sum(xh * xh, axis=-1, keepdims=True) + EPS)
            if hh < GDN_HEADS:
                qc.append(xh * (inv * GDN_DK ** -0.5))
            else:
                kc.append(xh * inv)
        qs.append(qc)
        ks.append(kc)
        vs.append([a[:, 2 * GDN_QK + h * GDN_DV:2 * GDN_QK + (h + 1) * GDN_DV] for h in range(GDN_HEADS)])
        zs = zs_ref[rows_of(ci), :]
        logits = _dot3(zs, up_ref[...]) + ub_ref[...]
        g_all.append((jnp.minimum(logits, 0.0) - jnp.log1p(jnp.exp(-jnp.abs(logits)))) * (1.0 / GLA_TAU))
        beta_all.append(_sigmoid(zs))
        glog_all.append(-jnp.exp(al_ref[...]) * _softplus(zs + dtb_ref[...]))

    sk = lambda h: slice(h * GLA_DK, (h + 1) * GLA_DK)
    cd = [(ci, d) for ci in chunks for d in dirs]
    v_bf, qe, kd, kl, bl = {}, {}, {}, {}, {}
    for ci in chunks:
        za = za_ref[rows_of(ci), :]
        q = za[:, :GLA_QK] * (GLA_DK ** -0.5)
        k = za[:, GLA_QK:2 * GLA_QK]
        v_bf[ci] = [za[:, 2 * GLA_QK + h * GLA_DV:2 * GLA_QK + (h + 1) * GLA_DV].astype(BF16)
                    for h in range(GLA_HEADS)]
        for d in dirs:
            bcum = _cum(cum_bf[d], g_all[ci][:, d * GLA_QK:(d + 1) * GLA_QK], 2)
            bl[ci, d] = last[d](bcum)
            qe[ci, d] = (q * jnp.exp(bcum)).astype(BF16)
            kd[ci, d] = (k * jnp.exp(-bcum)).astype(BF16)
            kl[ci, d] = (k * jnp.exp(bl[ci, d] - bcum)).astype(BF16)
    hd_a = [(ci, d, h) for ci, d in cd for h in range(GLA_HEADS)]
    att = [jnp.where(incl[d], _dg(qe[ci, d][:, sk(h)], kd[ci, d][:, sk(h)], _NT), 0.0).astype(BF16)
           for ci, d, h in hd_a]
    oi_a = {key: _dg(att[i], v_bf[key[0]][key[2]]) for i, key in enumerate(hd_a)}
    c_a = [_dg(kl[ci, d][:, sk(h)], v_bf[ci][h], _TN) for ci, d, h in hd_a]
    for ci, d in cd:
        qe_ref[ci, d] = qe[ci, d]
        er_ref[ci, d] = jnp.exp(bl[ci, d])
    for i, (ci, d, h) in enumerate(hd_a):
        ca_ref[ci, d, :, h * GLA_DV:(h + 1) * GLA_DV] = c_a[i].astype(BF16)

    gam_all = {key: _cum(cum_bf[key[1]], glog_all[key[0]], 3) for key in cd}
    gam_t = {key: gam_all[key].T for key in cd}
    ch_h = [(ci, h) for ci in chunks for h in range(GDN_HEADS)]
    k_bf = {(ci, h): ks[ci][h].astype(BF16) for ci, h in ch_h}
    raw = {(ci, h): _dg(jnp.concatenate([ks[ci][h], qs[ci][h]], axis=0).astype(BF16), k_bf[ci, h], _NT)
           for ci, h in ch_h}
    hd_b = [(ci, d, h) for ci, d in cd for h in range(GDN_HEADS)]
    gam_c, beta_c, gl, dm = [], [], [], []
    for ci, d, h in hd_b:
        ld, lb = LANE_DEC + d * GDN_HEADS + h, LANE_BETA + d * GDN_HEADS + h
        gc = gam_all[ci, d][:, ld:ld + 1]
        gr = gam_t[ci, d][ld:ld + 1, :]
        gam_c.append(gc)
        beta_c.append(beta_all[ci][:, lb:lb + 1])
        gl.append(last[d](gc))
        dm.append(jnp.where(incl[d], jnp.exp(jnp.minimum(gc - gr, 0.0)), 0.0))
    n_b = len(hd_b)
    a_low = [jnp.where(strict[d], raw[ci, h][:CHUNK] * beta_c[i] * dm[i], 0.0) for i, (ci, d, h) in enumerate(hd_b)]
    aqk = [(raw[ci, h][CHUNK:] * dm[i]).astype(BF16) for i, (ci, d, h) in enumerate(hd_b)]
    blk_shift = INV_BLOCK.bit_length() - 1
    blk_f = (jnp.right_shift(r, blk_shift) == jnp.right_shift(cc, blk_shift)).astype(F32)
    tmat = _tri_inverse_many(a_low, eye_f, blk_f)
    egc = [jnp.exp(gc) for gc in gam_c]
    rhs = [jnp.concatenate([vs[ci][h] * beta_c[i], ks[ci][h] * (beta_c[i] * egc[i])], axis=1).astype(BF16)
           for i, (ci, d, h) in enumerate(hd_b)]
    uw = [_dg(tmat[i].astype(BF16), rhs[i]).astype(BF16) for i in range(n_b)]
    aw = [_dg(aqk[i], uw[i]) for i in range(n_b)]
    khat = [(ks[ci][h] * jnp.exp(gl[i] - gam_c[i])).astype(BF16) for i, (ci, d, h) in enumerate(hd_b)]
    pc = [_dg(khat[i], uw[i], _TN) for i in range(n_b)]
    for ci, d in cd:
        row_e = jnp.zeros((1, SMALL_W), F32)
        for h in range(GDN_HEADS):
            row_e = jnp.where(lane == h, jnp.exp(gl[hd_b.index((ci, d, h))]), row_e)
        eg_ref[ci, d] = row_e
    for i, (ci, d, h) in enumerate(hd_b):
        lhs_ref[ci, d, h, :GDN_DK] = pc[i][:, GDN_DV:].astype(BF16)
        lhs_ref[ci, d, h, GDN_DK:] = (qs[ci][h] * egc[i] - aw[i][:, GDN_DV:]).astype(BF16)
        cg_ref[ci, d, h] = pc[i][:, :GDN_DV].astype(BF16)

    @pl.when(jnp.logical_not(is_ctx))
    def _():
        for ci in chunks:
            for h in range(GLA_HEADS):
                oi_ref[rows_of(ci), h * GLA_DV:(h + 1) * GLA_DV] = oi_a[ci, 0, h] + oi_a[ci, 1, h]
            for h in range(GDN_HEADS):
                i0, i1 = hd_b.index((ci, 0, h)), hd_b.index((ci, 1, h))
                oi_ref[rows_of(ci), GLA_V + h * GDN_DV:GLA_V + (h + 1) * GDN_DV] = (
                    aw[i0][:, :GDN_DV] + aw[i1][:, :GDN_DV])


def _pre_call(zb, za, zs, cw, up_bd, ub, alv, dtbv, n_lat, ch):
    b, tot, _ = zb.shape
    n_all = tot // CHUNK
    nt_lat, nt_all = n_lat // ch, n_all // ch
    tr = ch * CHUNK

    def prev_blk(bi, t):
        return jnp.where(t >= nt_lat, jnp.maximum(t * ch - 1, n_lat), jnp.maximum(t * ch - 1, 0))

    def next_blk(bi, t):
        return jnp.where(t >= nt_lat, jnp.minimum((t + 1) * ch, n_all - 1), jnp.minimum((t + 1) * ch, n_lat - 1))

    kern = functools.partial(_pre_kernel, nt_lat=nt_lat, nt_all=nt_all, ch=ch)
    full = lambda shape: pl.BlockSpec(shape, lambda bi, t: (0,) * len(shape))
    per_tile = lambda *tail: pl.BlockSpec((None, ch) + tail, lambda bi, t: (bi, t) + (0,) * len(tail))
    return pl.pallas_call(
        kern,
        grid=(b, nt_all),
        in_specs=[
            pl.BlockSpec((None, CHUNK, ZB_W), lambda bi, t: (bi, prev_blk(bi, t), 0)),
            pl.BlockSpec((None, tr, ZB_W), lambda bi, t: (bi, t, 0)),
            pl.BlockSpec((None, CHUNK, ZB_W), lambda bi, t: (bi, next_blk(bi, t), 0)),
            pl.BlockSpec((None, tr, ZA_W), lambda bi, t: (bi, t, 0)),
            pl.BlockSpec((None, tr, SMALL_W), lambda bi, t: (bi, t, 0)),
            full(cw.shape), full(up_bd.shape), full(ub.shape), full(alv.shape), full(dtbv.shape),
        ],
        out_specs=[
            per_tile(N_DIR, GDN_HEADS, GDN_DK + CHUNK, GDN_DV),
            per_tile(N_DIR, GDN_HEADS, GDN_DK, GDN_DV),
            per_tile(N_DIR, 1, SMALL_W),
            per_tile(N_DIR, CHUNK, GLA_QK),
            per_tile(N_DIR, CHUNK, GLA_V),
            per_tile(N_DIR, 1, GLA_QK),
            pl.BlockSpec((None, tr, O_W), lambda bi, t: (bi, jnp.minimum(t, nt_lat - 1), 0)),
        ],
        out_shape=[
            jax.ShapeDtypeStruct((b, n_all, N_DIR, GDN_HEADS, GDN_DK + CHUNK, GDN_DV), BF16),
            jax.ShapeDtypeStruct((b, n_all, N_DIR, GDN_HEADS, GDN_DK, GDN_DV), BF16),
            jax.ShapeDtypeStruct((b, n_all, N_DIR, 1, SMALL_W), F32),
            jax.ShapeDtypeStruct((b, n_all, N_DIR, CHUNK, GLA_QK), BF16),
            jax.ShapeDtypeStruct((b, n_all, N_DIR, CHUNK, GLA_V), BF16),
            jax.ShapeDtypeStruct((b, n_all, N_DIR, 1, GLA_QK), F32),
            jax.ShapeDtypeStruct((b, n_lat * CHUNK, O_W), F32),
        ],
        scratch_shapes=[pltpu.VMEM((tr, ZB_W), F32)],
        compiler_params=pltpu.CompilerParams(
            dimension_semantics=("arbitrary", "arbitrary"), vmem_limit_bytes=V7X_VMEM_LIMIT),
        name="chunk_pre",
    )(zb, zb, zb, za, zs, cw, up_bd, ub, alv, dtbv)


def _scan_kernel(*refs, ch):
    ins, (of_ref, ob_ref, sa_ref, sb_ref) = refs[:12], refs[12:]
    per_dir = [ins[0:6], ins[6:12]]
    o_refs = [of_ref, ob_ref]
    j = pl.program_id(1)

    @pl.when(j == 0)
    def _():
        sa_ref[...] = jnp.zeros_like(sa_ref)
        sb_ref[...] = jnp.zeros_like(sb_ref)

    r = lax.broadcasted_iota(jnp.int32, (GLA_DK, GLA_DK), 0)
    cc = lax.broadcasted_iota(jnp.int32, (GLA_DK, GLA_DK), 1)
    eye = cc == r

    hd = [(d, h) for d in range(N_DIR) for h in range(GDN_HEADS)]
    s_b = [sb_ref[d, h] for d, h in hd]
    s_a = [sa_ref[d, h] for d, h in hd]
    for s in range(ch):
        ci = [s, ch - 1 - s]
        rows = [slice(c * CHUNK, (c + 1) * CHUNK) for c in ci]
        res_b = [_dg(per_dir[d][0][ci[d], h], s_b[i].astype(BF16)) for i, (d, h) in enumerate(hd)]
        for i, (d, h) in enumerate(hd):
            e = per_dir[d][2][ci[d], :, h:h + 1]
            s_b[i] = s_b[i] * e - res_b[i][:GDN_DK] + per_dir[d][1][ci[d], h].astype(F32)
            o_refs[d][rows[d], GLA_V + h * GDN_DV:GLA_V + (h + 1) * GDN_DV] = res_b[i][GDN_DK:]
        res_a = [_dg(per_dir[d][3][ci[d], :, h * GLA_DK:(h + 1) * GLA_DK], s_a[i].astype(BF16))
                 for i, (d, h) in enumerate(hd)]
        for i, (d, h) in enumerate(hd):
            e_row = per_dir[d][5][ci[d], :, h * GLA_DK:(h + 1) * GLA_DK]
            ecol = jnp.sum(jnp.where(eye, e_row, 0.0), axis=1, keepdims=True)
            s_a[i] = s_a[i] * ecol + per_dir[d][4][ci[d], :, h * GLA_DV:(h + 1) * GLA_DV].astype(F32)
            o_refs[d][rows[d], h * GLA_DV:(h + 1) * GLA_DV] = res_a[i]
    for i, (d, h) in enumerate(hd):
        sb_ref[d, h] = s_b[i]
        sa_ref[d, h] = s_a[i]


def _scan_call(lhs, cg, eg, qe, ca, er, n_lat, ch):
    b, n_all = lhs.shape[0], lhs.shape[1]
    nt_all, nt_lat = n_all // ch, n_lat // ch
    nt_ctx = nt_all - nt_lat
    tr = ch * CHUNK

    def in_blk(dd, j):
        if dd == 0:
            return jnp.where(j < nt_ctx, nt_lat + j, j - nt_ctx)
        return nt_all - 1 - j

    def out_blk(dd, j):
        jj = jnp.maximum(j, nt_ctx)
        return jj - nt_ctx if dd == 0 else nt_all - 1 - jj

    def spec(arr, dd):
        tail = arr.shape[3:]
        return pl.BlockSpec((None, ch, None) + tail,
                            lambda bi, j: (bi, in_blk(dd, j), dd) + (0,) * len(tail))

    arrs = (lhs, cg, eg, qe, ca, er)
    in_specs = [spec(a, dd) for dd in range(N_DIR) for a in arrs]
    out_spec = lambda dd: pl.BlockSpec((None, tr, O_W), lambda bi, j: (bi, out_blk(dd, j), 0))
    o_shape = jax.ShapeDtypeStruct((b, n_lat * CHUNK, O_W), F32)
    return pl.pallas_call(
        functools.partial(_scan_kernel, ch=ch),
        grid=(b, nt_all),
        in_specs=in_specs,
        out_specs=[out_spec(0), out_spec(1)],
        out_shape=[o_shape, o_shape],
        scratch_shapes=[
            pltpu.VMEM((N_DIR, GLA_HEADS, GLA_DK, GLA_DV), F32),
            pltpu.VMEM((N_DIR, GDN_HEADS, GDN_DK, GDN_DV), F32),
        ],
        compiler_params=pltpu.CompilerParams(
            dimension_semantics=("arbitrary", "arbitrary"), vmem_limit_bytes=V7X_VMEM_LIMIT),
        name="state_scan",
    )(*(arrs + arrs))


def _merge_kernel(oi_ref, of_ref, ob_ref, zg_ref, x_ref, mod_ref, wa_ref, wb_ref, wo_ref, ga_ref, gb_ref,
                  bg_ref, fg_ref, y_ref, *, d_model):
    o = oi_ref[...] + of_ref[...] + ob_ref[...]
    zg = zg_ref[...]

    def head_norm(xh, gain):
        ms = jnp.mean(xh * xh, axis=-1, keepdims=True)
        return xh * lax.rsqrt(ms + EPS) * gain

    ya = jnp.concatenate(
        [head_norm(o[:, h * GLA_DV:(h + 1) * GLA_DV], ga_ref[...]) for h in range(GLA_HEADS)], axis=1)
    yb = jnp.concatenate(
        [head_norm(o[:, GLA_V + h * GDN_DV:GLA_V + (h + 1) * GDN_DV], gb_ref[...]) for h in range(GDN_HEADS)],
        axis=1)
    ya = ya * _silu(zg[:, :GLA_V])
    yb = yb * _silu(zg[:, GLA_V:GLA_V + GDN_V])
    pa = _dg(ya.astype(BF16), wa_ref[...])
    pb = _dg(yb.astype(BF16), wb_ref[...])
    gates = _sigmoid(zg[:, ZG_W:] + bg_ref[...])
    merged = gates[:, :d_model] * pa + gates[:, d_model:] * pb
    out = _dg(merged.astype(BF16), wo_ref[...])
    gate = mod_ref[...][:, 2 * d_model:]
    xn = x_ref[...] + gate * out
    ms = jnp.mean(xn * xn, axis=-1, keepdims=True)
    y_ref[...] = xn * lax.rsqrt(ms + EPS) * fg_ref[...]


def _merge_call(oi, of, ob, zg, x, mod3, wa, wb, wo, ga, gb, bg, fg):
    b, seq, d = x.shape
    tm = 256
    kern = functools.partial(_merge_kernel, d_model=d)
    full = lambda a: pl.BlockSpec(a.shape, lambda bi, t: (0,) * a.ndim)
    rows = lambda width: pl.BlockSpec((None, tm, width), lambda bi, t: (bi, t, 0))
    return pl.pallas_call(
        kern,
        grid=(b, seq // tm),
        in_specs=[
            rows(O_W), rows(O_W), rows(O_W), rows(zg.shape[2]), rows(d),
            pl.BlockSpec((None, 1, 3 * d), lambda bi, t: (bi, 0, 0)),
            full(wa), full(wb), full(wo), full(ga), full(gb), full(bg), full(fg),
        ],
        out_specs=rows(d),
        out_shape=jax.ShapeDtypeStruct((b, seq, d), x.dtype),
        compiler_params=pltpu.CompilerParams(
            dimension_semantics=("arbitrary", "arbitrary"), vmem_limit_bytes=V7X_VMEM_LIMIT),
        name="merge_out",
    )(oi, of, ob, zg, x, mod3, wa, wb, wo, ga, gb, bg, fg)


def _regroup_w_in(w):
    d = w.shape[0]
    o = 0
    offs = []
    for size in (GLA_QK, GLA_QK, GLA_V, GLA_V, N_DIR * GLA_RANK, GDN_QK, GDN_QK, GDN_V, GDN_V,
                 N_DIR * GDN_HEADS, N_DIR * GDN_HEADS, 2 * d):
        offs.append((o, o + size))
        o += size
    aq, ak, av, ag, alr, bq, bk, bv, bg, bbeta, bdec, mg = [w[:, s:e] for s, e in offs]
    small = jnp.concatenate([alr, bbeta, bdec], axis=1)
    small = jnp.pad(small, ((0, 0), (0, SMALL_W - small.shape[1])))
    return jnp.concatenate([aq, ak, av, bq, bk, bv, ag, bg, mg, small], axis=1).astype(BF16)


def kernel(x, c, ctx, c_ctx, w_mod, b_mod, norm_g, w_in, gla_up, gla_ub, gla_onorm, gdn_conv, gdn_a_log,
           gdn_dt_bias, gdn_onorm, w_gla_out, w_gdn_out, b_gate, w_o, final_g):
    b, seq, d = x.shape
    assert w_mod.shape[0] == 1, "single-layer block"
    assert seq % CHUNK == 0 and ctx.shape[1] % CHUNK == 0 and seq % ctx.shape[1] == 0
    n_lat = seq // CHUNK
    ch = _tile_chunks(n_lat, ctx.shape[1] // CHUNK)

    rows = -(-(b + 1) // 8) * 8
    cc = jnp.zeros((rows, d), F32).at[:b].set(c).at[b].set(c_ctx)
    mod = _mod_call(cc, w_mod[0], b_mod[0][None, :])
    mod3 = mod[:, None, :]

    w_r = _regroup_w_in(w_in[0])
    za, zb, zg, zs = _proj_call(x, ctx, mod3, norm_g[0][None, :], w_r)

    cw = gdn_conv[0].reshape(9, CONV_CH)
    up_bd = jnp.zeros((SMALL_W, N_DIR * GLA_QK), F32)
    for n in range(N_DIR):
        up_bd = up_bd.at[n * GLA_RANK:(n + 1) * GLA_RANK, n * GLA_QK:(n + 1) * GLA_QK].set(gla_up[0, n])
    ub = gla_ub[0].reshape(1, N_DIR * GLA_QK)
    nd = N_DIR * GDN_HEADS
    alv = jnp.zeros((1, SMALL_W), F32).at[0, LANE_DEC:LANE_DEC + nd].set(gdn_a_log[0].reshape(nd))
    dtbv = jnp.zeros((1, SMALL_W), F32).at[0, LANE_DEC:LANE_DEC + nd].set(gdn_dt_bias[0].reshape(nd))
    lhs, cg, eg, qe, ca, er, oi = _pre_call(zb, za, zs, cw, up_bd, ub, alv, dtbv, n_lat, ch)

    of, ob = _scan_call(lhs, cg, eg, qe, ca, er, n_lat, ch)

    return _merge_call(oi, of, ob, zg, x, mod3, w_gla_out[0].astype(BF16), w_gdn_out[0].astype(BF16),
                       w_o[0].astype(BF16), gla_onorm[0][None, :], gdn_onorm[0][None, :],
                       b_gate[0][None, :], final_g[None, :])
```

```python
import functools

import jax
import jax.numpy as jnp
from jax import lax
from jax.experimental import pallas as pl
from jax.experimental.pallas import tpu as pltpu

F32 = jnp.float32
BF16 = jnp.bfloat16

CHUNK = 64
N_DIR = 2
GLA_HEADS, GLA_DK, GLA_DV, GLA_RANK, GLA_TAU = 4, 64, 128, 16, 16.0
GLA_QK, GLA_V = GLA_HEADS * GLA_DK, GLA_HEADS * GLA_DV
GDN_HEADS, GDN_DK, GDN_DV = 4, 128, 128
GDN_QK, GDN_V = GDN_HEADS * GDN_DK, GDN_HEADS * GDN_DV
CONV_CH = 2 * GDN_QK + GDN_V
EPS = 1e-6
O_W = GLA_V + GDN_V

LANE_ALR = 0
LANE_BETA = N_DIR * GLA_RANK
LANE_DEC = LANE_BETA + N_DIR * GDN_HEADS
SMALL_W = 128

ZA_W = 2 * GLA_QK + GLA_V
ZB_W = CONV_CH
ZG_W = GLA_V + GDN_V

V7X_VMEM_LIMIT = 56 * 1024 * 1024

_NN = (((1,), (0,)), ((), ()))
_NT = (((1,), (1,)), ((), ()))
_TN = (((0,), (0,)), ((), ()))


def _dg(a, b, dims=_NN):
    return lax.dot_general(a, b, dims, preferred_element_type=F32)


def _dot3(a, b, dims=_NN):
    ah = a.astype(BF16)
    al = (a - ah.astype(F32)).astype(BF16)
    bh = b.astype(BF16)
    bl = (b - bh.astype(F32)).astype(BF16)
    return _dg(ah, bh, dims) + _dg(ah, bl, dims) + _dg(al, bh, dims)


def _cum(mask_bf, x, passes):
    acc = None
    rem = x
    for _ in range(passes):
        piece = rem.astype(BF16)
        term = _dg(mask_bf, piece)
        acc = term if acc is None else acc + term
        rem = rem - piece.astype(F32)
    return acc


def _sigmoid(x):
    return 1.0 / (1.0 + jnp.exp(-x))


def _silu(x):
    return x * _sigmoid(x)


def _softplus(x):
    return jnp.maximum(x, 0.0) + jnp.log1p(jnp.exp(-jnp.abs(x)))


def _tile_chunks(n_lat, n_ctx):
    for ch in (4, 2, 1):
        if n_lat % ch == 0 and n_ctx % ch == 0:
            return ch


def _mod_kernel(c_ref, w_ref, b_ref, o_ref):
    o_ref[...] = _dot3(_silu(c_ref[...]), w_ref[...]) + b_ref[...]


def _mod_call(cc, w_mod, b_mod):
    rows, d = cc.shape
    n = w_mod.shape[1]
    bn = d
    return pl.pallas_call(
        _mod_kernel,
        grid=(n // bn,),
        in_specs=[
            pl.BlockSpec((rows, d), lambda i: (0, 0)),
            pl.BlockSpec((d, bn), lambda i: (0, i)),
            pl.BlockSpec((1, bn), lambda i: (0, i)),
        ],
        out_specs=pl.BlockSpec((rows, bn), lambda i: (0, i)),
        out_shape=jax.ShapeDtypeStruct((rows, n), F32),
        compiler_params=pltpu.CompilerParams(dimension_semantics=("arbitrary",), vmem_limit_bytes=V7X_VMEM_LIMIT),
        name="adaln_mod",
    )(cc, w_mod, b_mod)


def _proj_kernel(x_ref, ctx_ref, mod_ref, g_ref, w_ref, za_ref, zb_ref, zg_ref, zs_ref, *, d_model):
    t = pl.program_id(1)
    xin = jnp.where(t == 0, ctx_ref[...], x_ref[...])
    ms = jnp.mean(xin * xin, axis=-1, keepdims=True)
    y = xin * lax.rsqrt(ms + EPS) * g_ref[...]
    mod = mod_ref[...]
    shift = mod[:, :d_model]
    scale = mod[:, d_model:2 * d_model]
    h = (y * (1.0 + scale) + shift).astype(BF16)
    o0, o1, o2, o3 = 0, ZA_W, ZA_W + ZB_W, ZA_W + ZB_W + ZG_W + 2 * d_model
    za_ref[...] = _dg(h, w_ref[:, o0:o1])
    zb_ref[...] = _dg(h, w_ref[:, o1:o2])
    zs_ref[...] = _dg(h, w_ref[:, o3:o3 + SMALL_W])

    @pl.when(t > 0)
    def _():
        zg_ref[...] = _dg(h, w_ref[:, o2:o3]).astype(zg_ref.dtype)


def _proj_call(x, ctx, mod3, norm_g, w_r):
    b, seq, d = x.shape
    ctx_len = ctx.shape[1]
    tm = ctx_len
    n_lat = seq // tm
    tot = seq + ctx_len
    zgw = ZG_W + 2 * d
    n_all = w_r.shape[1]

    def row_blk(bi, t):
        return jnp.where(t == 0, n_lat, t - 1)

    kern = functools.partial(_proj_kernel, d_model=d)
    return pl.pallas_call(
        kern,
        grid=(b, n_lat + 1),
        in_specs=[
            pl.BlockSpec((None, tm, d), lambda bi, t: (bi, jnp.maximum(t - 1, 0), 0)),
            pl.BlockSpec((None, tm, d), lambda bi, t: (bi, 0, 0)),
            pl.BlockSpec((None, 1, 3 * d), lambda bi, t: (jnp.where(t == 0, b, bi), 0, 0)),
            pl.BlockSpec((1, d), lambda bi, t: (0, 0)),
            pl.BlockSpec((d, n_all), lambda bi, t: (0, 0), pipeline_mode=pl.Buffered(1)),
        ],
        out_specs=[
            pl.BlockSpec((None, tm, ZA_W), lambda bi, t: (bi, row_blk(bi, t), 0)),
            pl.BlockSpec((None, tm, ZB_W), lambda bi, t: (bi, row_blk(bi, t), 0)),
            pl.BlockSpec((None, tm, zgw), lambda bi, t: (bi, jnp.maximum(t - 1, 0), 0)),
            pl.BlockSpec((None, tm, SMALL_W), lambda bi, t: (bi, row_blk(bi, t), 0)),
        ],
        out_shape=[
            jax.ShapeDtypeStruct((b, tot, ZA_W), F32),
            jax.ShapeDtypeStruct((b, tot, ZB_W), F32),
            jax.ShapeDtypeStruct((b, seq, zgw), BF16),
            jax.ShapeDtypeStruct((b, tot, SMALL_W), F32),
        ],
        compiler_params=pltpu.CompilerParams(
            dimension_semantics=("arbitrary", "arbitrary"), vmem_limit_bytes=V7X_VMEM_LIMIT),
        name="in_proj",
    )(x, ctx, mod3, norm_g, w_r)


INV_BLOCK = 16
FRONT_PERIOD = 14


def _stage(fn, items):
    out = []
    for it in items:
        out.append(fn(*it) if isinstance(it, tuple) else fn(it))
        yield
    return out


def _diag_block_inverses(neg_a_list, eye_f, blk_f):
    nds = [na * blk_f for na in neg_a_list]
    naos = [(na - nd).astype(BF16) for na, nd in zip(neg_a_list, nds)]
    ds = [eye_f + nd for nd in nds]
    ns = nds
    for _ in range(INV_BLOCK.bit_length() - 2):
        ns = yield from _stage(lambda n: _dg(n.astype(BF16), n.astype(BF16)), ns)
        ds = yield from _stage(lambda d, n: d + _dg(d.astype(BF16), n.astype(BF16)), list(zip(ds, ns)))
    return [d.astype(BF16) for d in ds], naos


def _finish_inverses(dbs, naos, eye_f):
    ms = yield from _stage(_dg, list(zip(dbs, naos)))
    mbs = [m.astype(BF16) for m in ms]
    m2s = yield from _stage(lambda mb: _dg(mb, mb), mbs)
    m3s = yield from _stage(lambda mb, m2: _dg(mb, m2.astype(BF16)), list(zip(mbs, m2s)))
    ss = [(eye_f + m + m2 + m3).astype(BF16) for m, m2, m3 in zip(ms, m2s, m3s)]
    return (yield from _stage(_dg, list(zip(ss, dbs))))


def _pre_kernel(zp_ref, zc_ref, zn_ref, zs_ref, za_ref, cw_ref, up_ref, ub_ref, al_ref, dtb_ref,
                lhs_ref, cg_ref, eg_ref, qe_ref, ca_ref, er_ref, oi_ref, act_ref, gdec_ref, small_ref,
                *, nt_lat, nt_all, n_tiles, ch):
    g = pl.program_id(0)
    tf = lax.rem(jnp.minimum(g, n_tiles - 1), nt_all)
    slot_f = lax.rem(g, 2)
    slot_b = 1 - slot_f
    rows_of = lambda ci: slice(ci * CHUNK, (ci + 1) * CHUNK)
    dirs = range(N_DIR)
    chunks = range(ch)

    @pl.when(g == 0)
    def _():
        act_ref[1] = jnp.zeros(act_ref.shape[1:], F32)
        gdec_ref[1] = jnp.zeros(gdec_ref.shape[1:], F32)
        small_ref[1] = jnp.zeros(small_ref.shape[1:], F32)

    f_ctx = tf >= nt_lat
    row = lax.broadcasted_iota(jnp.int32, (CHUNK, 1), 0)
    w = cw_ref[...]
    lat_f = jnp.logical_not(f_ctx).astype(F32)
    ctx_f = f_ctx.astype(F32)
    pv = (tf > jnp.where(f_ctx, nt_lat, 0)).astype(F32)
    nv = (tf < jnp.where(f_ctx, nt_all - 1, nt_lat - 1)).astype(F32)
    w_up, w_mid, w_dn = w[0:3] * lat_f, w[3:6], w[6:9] * lat_f
    w_edge_l, w_edge_r = w[3:4] * ctx_f, w[5:6] * ctx_f

    def front_cols(ci, j):
        cols = slice(j * GDN_DK, (j + 1) * GDN_DK)
        prev = zp_ref[:, cols] * pv if ci == 0 else zc_ref[rows_of(ci - 1), cols]
        nxt = zn_ref[:, cols] * nv if ci == ch - 1 else zc_ref[rows_of(ci + 1), cols]
        cur = zc_ref[rows_of(ci), cols]

        def col(dc):
            return (w_up[dc:dc + 1, cols] * prev + w_mid[dc:dc + 1, cols] * cur + w_dn[dc:dc + 1, cols] * nxt)

        left = jnp.where(row >= 1, pltpu.roll(col(0), 1, 0), w_edge_l[:, cols] * prev[CHUNK - 1:CHUNK])
        right = jnp.where(row <= CHUNK - 2, pltpu.roll(col(2), CHUNK - 1, 0), w_edge_r[:, cols] * nxt[0:1])
        a = _silu(col(1) + left + right)
        if j < 2 * GDN_HEADS:
            inv = lax.rsqrt(jnp.sum(a * a, axis=-1, keepdims=True) + EPS)
            a = a * (inv * (GDN_DK ** -0.5 if j < GDN_HEADS else 1.0))
        act_ref[slot_f, rows_of(ci), cols] = a

    def front_small(ci):
        zs = zs_ref[rows_of(ci), :]
        logits = _dot3(zs, up_ref[...]) + ub_ref[...]
        gdec_ref[slot_f, rows_of(ci), :] = (
            (jnp.minimum(logits, 0.0) - jnp.log1p(jnp.exp(-jnp.abs(logits)))) * (1.0 / GLA_TAU))
        small_ref[slot_f, 0, rows_of(ci), :] = _sigmoid(zs)
        small_ref[slot_f, 1, rows_of(ci), :] = -jnp.exp(al_ref[...]) * _softplus(zs + dtb_ref[...])

    pending_front = []
    for ci in chunks:
        pending_front.append(functools.partial(front_small, ci))
        pending_front.extend(functools.partial(front_cols, ci, j) for j in range(ZB_W // GDN_DK))
    ticks = [0]

    def tick():
        ticks[0] += 1
        if pending_front and ticks[0] % FRONT_PERIOD == 0:
            pending_front.pop(0)()

    def drain_front():
        while pending_front:
            pending_front.pop(0)()

    r = lax.broadcasted_iota(jnp.int32, (CHUNK, CHUNK), 0)
    cc = lax.broadcasted_iota(jnp.int32, (CHUNK, CHUNK), 1)
    incl = [cc <= r, cc >= r]
    strict_f = [(cc < r).astype(F32), (cc > r).astype(F32)]
    eye_f = (cc == r).astype(F32)
    blk_shift = INV_BLOCK.bit_length() - 1
    blk_f = (jnp.right_shift(r, blk_shift) == jnp.right_shift(cc, blk_shift)).astype(F32)
    cum_bf = [m.astype(BF16) for m in incl]
    last = [lambda x: x[CHUNK - 1:CHUNK], lambda x: x[0:1]]
    lane = lax.broadcasted_iota(jnp.int32, (1, SMALL_W), 1)
    sk = lambda h: slice(h * GLA_DK, (h + 1) * GLA_DK)

    def back():
        cs = list(chunks)
        cd = [(ci, d) for ci in cs for d in dirs]
        za = {ci: za_ref[rows_of(ci), :] for ci in cs}
        v_bf = {ci: [za[ci][:, 2 * GLA_QK + h * GLA_DV:2 * GLA_QK + (h + 1) * GLA_DV].astype(BF16)
                     for h in range(GLA_HEADS)] for ci in cs}

        def gla_front(ci, d):
            q = za[ci][:, :GLA_QK] * (GLA_DK ** -0.5)
            k = za[ci][:, GLA_QK:2 * GLA_QK]
            bcum = _cum(cum_bf[d], gdec_ref[slot_b, rows_of(ci), d * GLA_QK:(d + 1) * GLA_QK], 2)
            bl = last[d](bcum)
            qe = (q * jnp.exp(bcum)).astype(BF16)
            qe_ref[ci, d] = qe
            er_ref[ci, d] = jnp.exp(bl)
            return qe, (k * jnp.exp(-bcum)).astype(BF16), (k * jnp.exp(bl - bcum)).astype(BF16)

        qkk = dict(zip(cd, (yield from _stage(gla_front, cd))))
        hd_a = [(ci, d, h) for ci, d in cd for h in range(GLA_HEADS)]
        att = yield from _stage(
            lambda ci, d, h: jnp.where(incl[d], _dg(qkk[ci, d][0][:, sk(h)], qkk[ci, d][1][:, sk(h)], _NT),
                                       0.0).astype(BF16), hd_a)
        oi_a = dict(zip(hd_a, (yield from _stage(lambda i: _dg(att[i], v_bf[hd_a[i][0]][hd_a[i][2]]),
                                                  list(range(len(hd_a)))))))

        def gla_c(ci, d, h):
            ca_ref[ci, d, :, h * GLA_DV:(h + 1) * GLA_DV] = _dg(qkk[ci, d][2][:, sk(h)], v_bf[ci][h], _TN).astype(BF16)

        yield from _stage(gla_c, hd_a)

        ch_h = [(ci, h) for ci in cs for h in range(GDN_HEADS)]
        qs = {(ci, h): act_ref[slot_b, rows_of(ci), h * GDN_DK:(h + 1) * GDN_DK] for ci, h in ch_h}
        ks = {(ci, h): act_ref[slot_b, rows_of(ci), GDN_QK + h * GDN_DK:GDN_QK + (h + 1) * GDN_DK] for ci, h in ch_h}
        vs = {(ci, h): act_ref[slot_b, rows_of(ci), 2 * GDN_QK + h * GDN_DV:2 * GDN_QK + (h + 1) * GDN_DV]
              for ci, h in ch_h}
        beta_all = {ci: small_ref[slot_b, 0, rows_of(ci), :] for ci in cs}
        gam_all = dict(zip(cd, (yield from _stage(
            lambda ci, d: _cum(cum_bf[d], small_ref[slot_b, 1, rows_of(ci), :], 3), cd))))
        gam_t = {key: gam_all[key].T for key in cd}
        raw = dict(zip(ch_h, (yield from _stage(
            lambda ci, h: _dg(jnp.concatenate([ks[ci, h], qs[ci, h]], axis=0).astype(BF16),
                              ks[ci, h].astype(BF16), _NT), ch_h))))
        hd_b = [(ci, d, h) for ci, d in cd for h in range(GDN_HEADS)]
        n_b = len(hd_b)

        def decay_terms(ci, d, h):
            ld, lb = LANE_DEC + d * GDN_HEADS + h, LANE_BETA + d * GDN_HEADS + h
            gc = gam_all[ci, d][:, ld:ld + 1]
            gr = gam_t[ci, d][ld:ld + 1, :]
            dmask = jnp.where(incl[d], jnp.exp(jnp.minimum(gc - gr, 0.0)), 0.0)
            return gc, beta_all[ci][:, lb:lb + 1], last[d](gc), dmask

        terms = yield from _stage(decay_terms, hd_b)
        gam_c, beta_c, gl, dm = [list(x) for x in zip(*terms)]
        neg_a = yield from _stage(
            lambda i: raw[hd_b[i][0], hd_b[i][2]][:CHUNK] * (-beta_c[i]) * (dm[i] * strict_f[hd_b[i][1]]),
            list(range(n_b)))
        aqk = [(raw[ci, h][CHUNK:] * dm[i]).astype(BF16) for i, (ci, d, h) in enumerate(hd_b)]
        dbs, naos = yield from _diag_block_inverses(neg_a, eye_f, blk_f)
        tmat = yield from _finish_inverses(dbs, naos, eye_f)
        egc = [jnp.exp(gc) for gc in gam_c]
        uw = yield from _stage(
            lambda i: _dg(tmat[i].astype(BF16), jnp.concatenate(
                [vs[hd_b[i][0], hd_b[i][2]] * beta_c[i], ks[hd_b[i][0], hd_b[i][2]] * (beta_c[i] * egc[i])],
                axis=1).astype(BF16)).astype(BF16), list(range(n_b)))
        aw = yield from _stage(lambda i: _dg(aqk[i], uw[i]), list(range(n_b)))
        pc = yield from _stage(
            lambda i: _dg((ks[hd_b[i][0], hd_b[i][2]] * jnp.exp(gl[i] - gam_c[i])).astype(BF16), uw[i], _TN),
            list(range(n_b)))
        for ci, d in cd:
            row_e = jnp.zeros((1, SMALL_W), F32)
            for h in range(GDN_HEADS):
                row_e = jnp.where(lane == h, jnp.exp(gl[hd_b.index((ci, d, h))]), row_e)
            eg_ref[ci, d] = row_e

        def store_chain(i):
            ci, d, h = hd_b[i]
            lhs_ref[ci, d, h, :GDN_DK] = pc[i][:, GDN_DV:].astype(BF16)
            lhs_ref[ci, d, h, GDN_DK:] = (qs[ci, h] * egc[i] - aw[i][:, GDN_DV:]).astype(BF16)
            cg_ref[ci, d, h] = pc[i][:, :GDN_DV].astype(BF16)

        yield from _stage(store_chain, list(range(n_b)))
        for ci in cs:
            for h in range(GLA_HEADS):
                oi_ref[rows_of(ci), h * GLA_DV:(h + 1) * GLA_DV] = oi_a[ci, 0, h] + oi_a[ci, 1, h]
            for h in range(GDN_HEADS):
                i0, i1 = hd_b.index((ci, 0, h)), hd_b.index((ci, 1, h))
                oi_ref[rows_of(ci), GLA_V + h * GDN_DV:GLA_V + (h + 1) * GDN_DV] = (
                    aw[i0][:, :GDN_DV] + aw[i1][:, :GDN_DV])
            yield

    for _ in back():
        tick()
    drain_front()


def _pre_call(zb, za, zs, cw, up_bd, ub, alv, dtbv, n_lat, ch):
    b, tot, _ = zb.shape
    n_all = tot // CHUNK
    nt_lat, nt_all = n_lat // ch, n_all // ch
    n_tiles = b * nt_all
    tr = ch * CHUNK

    def front_tile(g):
        gf = jnp.minimum(g, n_tiles - 1)
        return lax.div(gf, nt_all), lax.rem(gf, nt_all)

    def back_tile(g):
        gb = jnp.maximum(g - 1, 0)
        return lax.div(gb, nt_all), lax.rem(gb, nt_all)

    def prev_blk(g):
        bi, t = front_tile(g)
        return bi, jnp.where(t >= nt_lat, jnp.maximum(t * ch - 1, n_lat), jnp.maximum(t * ch - 1, 0)), 0

    def next_blk(g):
        bi, t = front_tile(g)
        return bi, jnp.where(t >= nt_lat, jnp.minimum((t + 1) * ch, n_all - 1),
                             jnp.minimum((t + 1) * ch, n_lat - 1)), 0

    kern = functools.partial(_pre_kernel, nt_lat=nt_lat, nt_all=nt_all, n_tiles=n_tiles, ch=ch)
    full = lambda shape: pl.BlockSpec(shape, lambda g: (0,) * len(shape))
    per_tile = lambda *tail: pl.BlockSpec((None, ch) + tail, lambda g: back_tile(g) + (0,) * len(tail))
    return pl.pallas_call(
        kern,
        grid=(n_tiles + 1,),
        in_specs=[
            pl.BlockSpec((None, CHUNK, ZB_W), prev_blk),
            pl.BlockSpec((None, tr, ZB_W), lambda g: front_tile(g) + (0,)),
            pl.BlockSpec((None, CHUNK, ZB_W), next_blk),
            pl.BlockSpec((None, tr, SMALL_W), lambda g: front_tile(g) + (0,)),
            pl.BlockSpec((None, tr, ZA_W), lambda g: back_tile(g) + (0,)),
            full(cw.shape), full(up_bd.shape), full(ub.shape), full(alv.shape), full(dtbv.shape),
        ],
        out_specs=[
            per_tile(N_DIR, GDN_HEADS, GDN_DK + CHUNK, GDN_DV),
            per_tile(N_DIR, GDN_HEADS, GDN_DK, GDN_DV),
            per_tile(N_DIR, 1, SMALL_W),
            per_tile(N_DIR, CHUNK, GLA_QK),
            per_tile(N_DIR, CHUNK, GLA_V),
            per_tile(N_DIR, 1, GLA_QK),
            pl.BlockSpec((None, tr, O_W), lambda g: back_tile(g) + (0,)),
        ],
        out_shape=[
            jax.ShapeDtypeStruct((b, n_all, N_DIR, GDN_HEADS, GDN_DK + CHUNK, GDN_DV), BF16),
            jax.ShapeDtypeStruct((b, n_all, N_DIR, GDN_HEADS, GDN_DK, GDN_DV), BF16),
            jax.ShapeDtypeStruct((b, n_all, N_DIR, 1, SMALL_W), F32),
            jax.ShapeDtypeStruct((b, n_all, N_DIR, CHUNK, GLA_QK), BF16),
            jax.ShapeDtypeStruct((b, n_all, N_DIR, CHUNK, GLA_V), BF16),
            jax.ShapeDtypeStruct((b, n_all, N_DIR, 1, GLA_QK), F32),
            jax.ShapeDtypeStruct((b, tot, O_W), F32),
        ],
        scratch_shapes=[
            pltpu.VMEM((2, tr, ZB_W), F32),
            pltpu.VMEM((2, tr, N_DIR * GLA_QK), F32),
            pltpu.VMEM((2, 2, tr, SMALL_W), F32),
        ],
        compiler_params=pltpu.CompilerParams(dimension_semantics=("arbitrary",), vmem_limit_bytes=V7X_VMEM_LIMIT),
        name="chunk_pre",
    )(zb, zb, zb, zs, za, cw, up_bd, ub, alv, dtbv)


def _scan_kernel(*refs, ch):
    ins, (of_ref, ob_ref, sa_ref, sb_ref) = refs[:12], refs[12:]
    per_dir = [ins[0:6], ins[6:12]]
    o_refs = [of_ref, ob_ref]
    j = pl.program_id(1)

    @pl.when(j == 0)
    def _():
        sa_ref[...] = jnp.zeros_like(sa_ref)
        sb_ref[...] = jnp.zeros_like(sb_ref)

    r = lax.broadcasted_iota(jnp.int32, (GLA_DK, GLA_DK), 0)
    cc = lax.broadcasted_iota(jnp.int32, (GLA_DK, GLA_DK), 1)
    eye = cc == r

    hd = [(d, h) for d in range(N_DIR) for h in range(GDN_HEADS)]
    s_b = [sb_ref[d, h] for d, h in hd]
    s_a = [sa_ref[d, h] for d, h in hd]
    for s in range(ch):
        ci = [s, ch - 1 - s]
        rows = [slice(c * CHUNK, (c + 1) * CHUNK) for c in ci]
        res_b = [_dg(per_dir[d][0][ci[d], h], s_b[i].astype(BF16)) for i, (d, h) in enumerate(hd)]
        for i, (d, h) in enumerate(hd):
            e = per_dir[d][2][ci[d], :, h:h + 1]
            s_b[i] = s_b[i] * e - res_b[i][:GDN_DK] + per_dir[d][1][ci[d], h].astype(F32)
            o_refs[d][rows[d], GLA_V + h * GDN_DV:GLA_V + (h + 1) * GDN_DV] = res_b[i][GDN_DK:].astype(BF16)
        res_a = [_dg(per_dir[d][3][ci[d], :, h * GLA_DK:(h + 1) * GLA_DK], s_a[i].astype(BF16))
                 for i, (d, h) in enumerate(hd)]
        for i, (d, h) in enumerate(hd):
            e_row = per_dir[d][5][ci[d], :, h * GLA_DK:(h + 1) * GLA_DK]
            ecol = jnp.sum(jnp.where(eye, e_row, 0.0), axis=1, keepdims=True)
            s_a[i] = s_a[i] * ecol + per_dir[d][4][ci[d], :, h * GLA_DV:(h + 1) * GLA_DV].astype(F32)
            o_refs[d][rows[d], h * GLA_DV:(h + 1) * GLA_DV] = res_a[i].astype(BF16)
    for i, (d, h) in enumerate(hd):
        sb_ref[d, h] = s_b[i]
        sa_ref[d, h] = s_a[i]


def _scan_call(lhs, cg, eg, qe, ca, er, n_lat, ch):
    b, n_all = lhs.shape[0], lhs.shape[1]
    nt_all, nt_lat = n_all // ch, n_lat // ch
    nt_ctx = nt_all - nt_lat
    tr = ch * CHUNK

    def in_blk(dd, j):
        if dd == 0:
            return jnp.where(j < nt_ctx, nt_lat + j, j - nt_ctx)
        return nt_all - 1 - j

    def out_blk(dd, j):
        jj = jnp.maximum(j, nt_ctx)
        return jj - nt_ctx if dd == 0 else nt_all - 1 - jj

    def spec(arr, dd):
        tail = arr.shape[3:]
        return pl.BlockSpec((None, ch, None) + tail,
                            lambda bi, j: (bi, in_blk(dd, j), dd) + (0,) * len(tail))

    arrs = (lhs, cg, eg, qe, ca, er)
    in_specs = [spec(a, dd) for dd in range(N_DIR) for a in arrs]
    out_spec = lambda dd: pl.BlockSpec((None, tr, O_W), lambda bi, j: (bi, out_blk(dd, j), 0))
    o_shape = jax.ShapeDtypeStruct((b, n_lat * CHUNK, O_W), BF16)
    return pl.pallas_call(
        functools.partial(_scan_kernel, ch=ch),
        grid=(b, nt_all),
        in_specs=in_specs,
        out_specs=[out_spec(0), out_spec(1)],
        out_shape=[o_shape, o_shape],
        scratch_shapes=[
            pltpu.VMEM((N_DIR, GLA_HEADS, GLA_DK, GLA_DV), F32),
            pltpu.VMEM((N_DIR, GDN_HEADS, GDN_DK, GDN_DV), F32),
        ],
        compiler_params=pltpu.CompilerParams(
            dimension_semantics=("arbitrary", "arbitrary"), vmem_limit_bytes=V7X_VMEM_LIMIT),
        name="state_scan",
    )(*(arrs + arrs))


def _merge_kernel(oi_ref, of_ref, ob_ref, zg_ref, x_ref, mod_ref, wa_ref, wb_ref, wo_ref, ga_ref, gb_ref,
                  bg_ref, fg_ref, y_ref, *, d_model):
    o = oi_ref[...] + of_ref[...].astype(F32) + ob_ref[...].astype(F32)
    zg = zg_ref[...].astype(F32)

    def head_norm(xh, gain):
        ms = jnp.mean(xh * xh, axis=-1, keepdims=True)
        return xh * lax.rsqrt(ms + EPS) * gain

    ya = jnp.concatenate(
        [head_norm(o[:, h * GLA_DV:(h + 1) * GLA_DV], ga_ref[...]) for h in range(GLA_HEADS)], axis=1)
    yb = jnp.concatenate(
        [head_norm(o[:, GLA_V + h * GDN_DV:GLA_V + (h + 1) * GDN_DV], gb_ref[...]) for h in range(GDN_HEADS)],
        axis=1)
    ya = ya * _silu(zg[:, :GLA_V])
    yb = yb * _silu(zg[:, GLA_V:GLA_V + GDN_V])
    pa = _dg(ya.astype(BF16), wa_ref[...])
    pb = _dg(yb.astype(BF16), wb_ref[...])
    gates = _sigmoid(zg[:, ZG_W:] + bg_ref[...])
    merged = gates[:, :d_model] * pa + gates[:, d_model:] * pb
    out = _dg(merged.astype(BF16), wo_ref[...])
    gate = mod_ref[...][:, 2 * d_model:]
    xn = x_ref[...] + gate * out
    ms = jnp.mean(xn * xn, axis=-1, keepdims=True)
    y_ref[...] = xn * lax.rsqrt(ms + EPS) * fg_ref[...]


def _merge_call(oi, of, ob, zg, x, mod3, wa, wb, wo, ga, gb, bg, fg):
    b, seq, d = x.shape
    tm = 256
    kern = functools.partial(_merge_kernel, d_model=d)
    full = lambda a: pl.BlockSpec(a.shape, lambda bi, t: (0,) * a.ndim)
    rows = lambda width: pl.BlockSpec((None, tm, width), lambda bi, t: (bi, t, 0))
    return pl.pallas_call(
        kern,
        grid=(b, seq // tm),
        in_specs=[
            rows(O_W), rows(O_W), rows(O_W), rows(zg.shape[2]), rows(d),
            pl.BlockSpec((None, 1, 3 * d), lambda bi, t: (bi, 0, 0)),
            full(wa), full(wb), full(wo), full(ga), full(gb), full(bg), full(fg),
        ],
        out_specs=rows(d),
        out_shape=jax.ShapeDtypeStruct((b, seq, d), x.dtype),
        compiler_params=pltpu.CompilerParams(
            dimension_semantics=("arbitrary", "arbitrary"), vmem_limit_bytes=V7X_VMEM_LIMIT),
        name="merge_out",
    )(oi, of, ob, zg, x, mod3, wa, wb, wo, ga, gb, bg, fg)


def _regroup_w_in(w):
    d = w.shape[0]
    o = 0
    offs = []
    for size in (GLA_QK, GLA_QK, GLA_V, GLA_V, N_DIR * GLA_RANK, GDN_QK, GDN_QK, GDN_V, GDN_V,
                 N_DIR * GDN_HEADS, N_DIR * GDN_HEADS, 2 * d):
        offs.append((o, o + size))
        o += size
    aq, ak, av, ag, alr, bq, bk, bv, bg, bbeta, bdec, mg = [w[:, s:e] for s, e in offs]
    small = jnp.concatenate([alr, bbeta, bdec], axis=1)
    small = jnp.pad(small, ((0, 0), (0, SMALL_W - small.shape[1])))
    return jnp.concatenate([aq, ak, av, bq, bk, bv, ag, bg, mg, small], axis=1).astype(BF16)


def kernel(x, c, ctx, c_ctx, w_mod, b_mod, norm_g, w_in, gla_up, gla_ub, gla_onorm, gdn_conv, gdn_a_log,
           gdn_dt_bias, gdn_onorm, w_gla_out, w_gdn_out, b_gate, w_o, final_g):
    b, seq, d = x.shape
    assert w_mod.shape[0] == 1, "single-layer block"
    assert seq % CHUNK == 0 and ctx.shape[1] % CHUNK == 0 and seq % ctx.shape[1] == 0
    n_lat = seq // CHUNK
    ch = _tile_chunks(n_lat, ctx.shape[1] // CHUNK)

    rows = -(-(b + 1) // 8) * 8
    cc = jnp.concatenate([c, c_ctx[None, :], jnp.zeros((rows - b - 1, d), F32)], axis=0)
    mod = _mod_call(cc, w_mod[0], b_mod[0][None, :])
    mod3 = mod[:, None, :]

    w_r = _regroup_w_in(w_in[0])
    za, zb, zg, zs = _proj_call(x, ctx, mod3, norm_g[0][None, :], w_r)

    cw = gdn_conv[0].reshape(9, CONV_CH)
    up_bd = jnp.concatenate(
        [jnp.pad(gla_up[0, n], ((0, 0), (n * GLA_QK, (N_DIR - 1 - n) * GLA_QK))) for n in range(N_DIR)], axis=0)
    up_bd = jnp.pad(up_bd, ((0, SMALL_W - N_DIR * GLA_RANK), (0, 0)))
    ub = gla_ub[0].reshape(1, N_DIR * GLA_QK)
    nd = N_DIR * GDN_HEADS
    lane_pad = ((0, 0), (LANE_DEC, SMALL_W - LANE_DEC - nd))
    alv = jnp.pad(gdn_a_log[0].reshape(1, nd), lane_pad)
    dtbv = jnp.pad(gdn_dt_bias[0].reshape(1, nd), lane_pad)
    lhs, cg, eg, qe, ca, er, oi = _pre_call(zb, za, zs, cw, up_bd, ub, alv, dtbv, n_lat, ch)

    of, ob = _scan_call(lhs, cg, eg, qe, ca, er, n_lat, ch)

    return _merge_call(oi, of, ob, zg, x, mod3, w_gla_out[0].astype(BF16), w_gdn_out[0].astype(BF16),
                       w_o[0].astype(BF16), gla_onorm[0][None, :], gdn_onorm[0][None, :],
                       b_gate[0][None, :], final_g[None, :])
```

```python
import functools

import jax
import jax.numpy as jnp
from jax import lax
from jax.experimental import pallas as pl
from jax.experimental.pallas import tpu as pltpu

F32 = jnp.float32
BF16 = jnp.bfloat16

CHUNK = 64
N_DIR = 2
GLA_HEADS, GLA_DK, GLA_DV, GLA_RANK, GLA_TAU = 4, 64, 128, 16, 16.0
GLA_QK, GLA_V = GLA_HEADS * GLA_DK, GLA_HEADS * GLA_DV
GDN_HEADS, GDN_DK, GDN_DV = 4, 128, 128
GDN_QK, GDN_V = GDN_HEADS * GDN_DK, GDN_HEADS * GDN_DV
CONV_CH = 2 * GDN_QK + GDN_V
EPS = 1e-6
GLA_FACTOR_LIMIT = 80.0
O_W = GLA_V + GDN_V

LANE_ALR = 0
LANE_BETA = N_DIR * GLA_RANK
LANE_DEC = LANE_BETA + N_DIR * GDN_HEADS
SMALL_W = 128

ZA_W = 2 * GLA_QK + GLA_V
ZB_W = CONV_CH
ZG_W = GLA_V + GDN_V

V7X_VMEM_LIMIT = 56 * 1024 * 1024

_NN = (((1,), (0,)), ((), ()))
_NT = (((1,), (1,)), ((), ()))
_TN = (((0,), (0,)), ((), ()))


def _dg(a, b, dims=_NN):
    return lax.dot_general(a, b, dims, preferred_element_type=F32)


def _dot3(a, b, dims=_NN):
    ah = a.astype(BF16)
    al = (a - ah.astype(F32)).astype(BF16)
    bh = b.astype(BF16)
    bl = (b - bh.astype(F32)).astype(BF16)
    return _dg(ah, bh, dims) + _dg(ah, bl, dims) + _dg(al, bh, dims)


def _cum(mask_bf, x, passes):
    acc = None
    rem = x
    for _ in range(passes):
        piece = rem.astype(BF16)
        term = _dg(mask_bf, piece)
        acc = term if acc is None else acc + term
        rem = rem - piece.astype(F32)
    return acc


def _sigmoid(x):
    return 1.0 / (1.0 + jnp.exp(-x))


def _silu(x):
    return x * _sigmoid(x)


def _softplus(x):
    return jnp.maximum(x, 0.0) + jnp.log1p(jnp.exp(-jnp.abs(x)))


def _tile_chunks(n_lat, n_ctx):
    for ch in (4, 2, 1):
        if n_lat % ch == 0 and n_ctx % ch == 0:
            return ch


def _mod_kernel(c_ref, w_ref, b_ref, o_ref):
    o_ref[...] = _dot3(_silu(c_ref[...]), w_ref[...]) + b_ref[...]


def _mod_call(cc, w_mod, b_mod):
    rows, d = cc.shape
    n = w_mod.shape[1]
    bn = d
    return pl.pallas_call(
        _mod_kernel,
        grid=(n // bn,),
        in_specs=[
            pl.BlockSpec((rows, d), lambda i: (0, 0)),
            pl.BlockSpec((d, bn), lambda i: (0, i)),
            pl.BlockSpec((1, bn), lambda i: (0, i)),
        ],
        out_specs=pl.BlockSpec((rows, bn), lambda i: (0, i)),
        out_shape=jax.ShapeDtypeStruct((rows, n), F32),
        compiler_params=pltpu.CompilerParams(dimension_semantics=("arbitrary",), vmem_limit_bytes=V7X_VMEM_LIMIT),
        name="adaln_mod",
    )(cc, w_mod, b_mod)


def _proj_kernel(x_ref, ctx_ref, mod_ref, g_ref, w_ref, za_ref, zb_ref, zg_ref, zs_ref, nrm_ref, *, d_model):
    t = pl.program_id(1)
    xin = jnp.where(t == 0, ctx_ref[...], x_ref[...])
    ms = jnp.mean(xin * xin, axis=-1, keepdims=True)
    y = xin * lax.rsqrt(ms + EPS) * g_ref[...]
    mod = mod_ref[...]
    shift = mod[:, :d_model]
    scale = mod[:, d_model:2 * d_model]
    h = (y * (1.0 + scale) + shift).astype(BF16)
    o0, o1, o2, o3 = 0, ZA_W, ZA_W + ZB_W, ZA_W + ZB_W + ZG_W + 2 * d_model
    zs = _dg(h, w_ref[:, o3:o3 + SMALL_W])
    zs_ref[...] = zs
    lane = lax.broadcasted_iota(jnp.int32, (1, SMALL_W), 1)
    sq = zs * zs
    nrm = jnp.zeros((1, SMALL_W), F32)
    for n in range(N_DIR):
        in_dir = (lane >= LANE_ALR + n * GLA_RANK) & (lane < LANE_ALR + (n + 1) * GLA_RANK)
        worst = jnp.max(jnp.sum(jnp.where(in_dir, sq, 0.0), axis=1, keepdims=True), axis=0, keepdims=True)
        nrm = jnp.where(lane == n, worst, nrm)
    nrm_ref[...] = jnp.broadcast_to(nrm, nrm_ref.shape)
    za_ref[...] = _dg(h, w_ref[:, o0:o1])
    zb_ref[...] = _dg(h, w_ref[:, o1:o2])

    @pl.when(t > 0)
    def _():
        zg_ref[...] = _dg(h, w_ref[:, o2:o3]).astype(zg_ref.dtype)


def _proj_call(x, ctx, mod3, norm_g, w_r):
    b, seq, d = x.shape
    ctx_len = ctx.shape[1]
    tm = ctx_len
    n_lat = seq // tm
    tot = seq + ctx_len
    zgw = ZG_W + 2 * d
    n_all = w_r.shape[1]

    def row_blk(bi, t):
        return jnp.where(t == 0, n_lat, t - 1)

    kern = functools.partial(_proj_kernel, d_model=d)
    return pl.pallas_call(
        kern,
        grid=(b, n_lat + 1),
        in_specs=[
            pl.BlockSpec((None, tm, d), lambda bi, t: (bi, jnp.maximum(t - 1, 0), 0)),
            pl.BlockSpec((None, tm, d), lambda bi, t: (bi, 0, 0)),
            pl.BlockSpec((None, 1, 3 * d), lambda bi, t: (jnp.where(t == 0, b, bi), 0, 0)),
            pl.BlockSpec((1, d), lambda bi, t: (0, 0)),
            pl.BlockSpec((d, n_all), lambda bi, t: (0, 0), pipeline_mode=pl.Buffered(1)),
        ],
        out_specs=[
            pl.BlockSpec((None, tm, ZA_W), lambda bi, t: (bi, row_blk(bi, t), 0)),
            pl.BlockSpec((None, tm, ZB_W), lambda bi, t: (bi, row_blk(bi, t), 0)),
            pl.BlockSpec((None, tm, zgw), lambda bi, t: (bi, jnp.maximum(t - 1, 0), 0)),
            pl.BlockSpec((None, tm, SMALL_W), lambda bi, t: (bi, row_blk(bi, t), 0)),
            pl.BlockSpec((None, None, 8, SMALL_W), lambda bi, t: (bi, t, 0, 0)),
        ],
        out_shape=[
            jax.ShapeDtypeStruct((b, tot, ZA_W), F32),
            jax.ShapeDtypeStruct((b, tot, ZB_W), F32),
            jax.ShapeDtypeStruct((b, seq, zgw), BF16),
            jax.ShapeDtypeStruct((b, tot, SMALL_W), F32),
            jax.ShapeDtypeStruct((b, n_lat + 1, 8, SMALL_W), F32),
        ],
        compiler_params=pltpu.CompilerParams(
            dimension_semantics=("arbitrary", "arbitrary"), vmem_limit_bytes=V7X_VMEM_LIMIT),
        name="in_proj",
    )(x, ctx, mod3, norm_g, w_r)


INV_BLOCK = 16
FRONT_PERIOD = 14


def _stage(fn, items):
    out = []
    for it in items:
        out.append(fn(*it) if isinstance(it, tuple) else fn(it))
        yield
    return out


def _diag_block_inverses(neg_a_list, eye_f, blk_f):
    nds = [na * blk_f for na in neg_a_list]
    naos = [(na - nd).astype(BF16) for na, nd in zip(neg_a_list, nds)]
    ds = [eye_f + nd for nd in nds]
    ns = nds
    for _ in range(INV_BLOCK.bit_length() - 2):
        ns = yield from _stage(lambda n: _dg(n.astype(BF16), n.astype(BF16)), ns)
        ds = yield from _stage(lambda d, n: d + _dg(d.astype(BF16), n.astype(BF16)), list(zip(ds, ns)))
    return [d.astype(BF16) for d in ds], naos


def _finish_inverses(dbs, naos, eye_f):
    ms = yield from _stage(_dg, list(zip(dbs, naos)))
    mbs = [m.astype(BF16) for m in ms]
    m2s = yield from _stage(lambda mb: _dg(mb, mb), mbs)
    m3s = yield from _stage(lambda mb, m2: _dg(mb, m2.astype(BF16)), list(zip(mbs, m2s)))
    ss = [(eye_f + m + m2 + m3).astype(BF16) for m, m2, m3 in zip(ms, m2s, m3s)]
    return (yield from _stage(_dg, list(zip(ss, dbs))))


def _gla_scores_exact(q, k, bcum, k_ref, b_ref):
    k_ref[...] = k
    b_ref[...] = bcum
    lane_s = lax.broadcasted_iota(jnp.int32, (1, CHUNK), 1)
    rr = lax.broadcasted_iota(jnp.int32, (GLA_QK, SMALL_W), 0)
    ll = lax.broadcasted_iota(jnp.int32, (GLA_QK, SMALL_W), 1)
    head_sum = (jnp.right_shift(rr, GLA_DK.bit_length() - 1) == ll).astype(F32)

    def column(s, acc):
        p = q * jnp.exp(jnp.minimum(bcum - b_ref[pl.ds(s, 1), :], 0.0)) * k_ref[pl.ds(s, 1), :]
        per_head = _dot3(p, head_sum)
        hit = (lane_s == s).astype(F32)
        return tuple(a + per_head[:, h:h + 1] * hit for h, a in enumerate(acc))

    zero = jnp.zeros((CHUNK, CHUNK), F32)
    return lax.fori_loop(0, CHUNK, column, (zero,) * GLA_HEADS)


def _pre_kernel(zp_ref, zc_ref, zn_ref, zs_ref, za_ref, cw_ref, up_ref, ub_ref, al_ref, dtb_ref,
                lhs_ref, cg_ref, eg_ref, qe_ref, ca_ref, er_ref, oi_ref, act_ref, gdec_ref, small_ref, *exact_scratch,
                nt_lat, nt_all, n_tiles, ch, exact_gla):
    g = pl.program_id(0)
    tf = lax.rem(jnp.minimum(g, n_tiles - 1), nt_all)
    slot_f = lax.rem(g, 2)
    slot_b = 1 - slot_f
    rows_of = lambda ci: slice(ci * CHUNK, (ci + 1) * CHUNK)
    dirs = range(N_DIR)
    chunks = range(ch)

    @pl.when(g == 0)
    def _():
        act_ref[1] = jnp.zeros(act_ref.shape[1:], F32)
        gdec_ref[1] = jnp.zeros(gdec_ref.shape[1:], F32)
        small_ref[1] = jnp.zeros(small_ref.shape[1:], F32)

    f_ctx = tf >= nt_lat
    row = lax.broadcasted_iota(jnp.int32, (CHUNK, 1), 0)
    w = cw_ref[...]
    lat_f = jnp.logical_not(f_ctx).astype(F32)
    ctx_f = f_ctx.astype(F32)
    pv = (tf > jnp.where(f_ctx, nt_lat, 0)).astype(F32)
    nv = (tf < jnp.where(f_ctx, nt_all - 1, nt_lat - 1)).astype(F32)
    w_up, w_mid, w_dn = w[0:3] * lat_f, w[3:6], w[6:9] * lat_f
    w_edge_l, w_edge_r = w[3:4] * ctx_f, w[5:6] * ctx_f

    def front_cols(ci, j):
        cols = slice(j * GDN_DK, (j + 1) * GDN_DK)
        prev = zp_ref[:, cols] * pv if ci == 0 else zc_ref[rows_of(ci - 1), cols]
        nxt = zn_ref[:, cols] * nv if ci == ch - 1 else zc_ref[rows_of(ci + 1), cols]
        cur = zc_ref[rows_of(ci), cols]

        def col(dc):
            return (w_up[dc:dc + 1, cols] * prev + w_mid[dc:dc + 1, cols] * cur + w_dn[dc:dc + 1, cols] * nxt)

        left = jnp.where(row >= 1, pltpu.roll(col(0), 1, 0), w_edge_l[:, cols] * prev[CHUNK - 1:CHUNK])
        right = jnp.where(row <= CHUNK - 2, pltpu.roll(col(2), CHUNK - 1, 0), w_edge_r[:, cols] * nxt[0:1])
        a = _silu(col(1) + left + right)
        if j < 2 * GDN_HEADS:
            inv = lax.rsqrt(jnp.sum(a * a, axis=-1, keepdims=True) + EPS)
            a = a * (inv * (GDN_DK ** -0.5 if j < GDN_HEADS else 1.0))
        act_ref[slot_f, rows_of(ci), cols] = a

    def front_small(ci):
        zs = zs_ref[rows_of(ci), :]
        logits = _dot3(zs, up_ref[...]) + ub_ref[...]
        gdec_ref[slot_f, rows_of(ci), :] = (
            (jnp.minimum(logits, 0.0) - jnp.log1p(jnp.exp(-jnp.abs(logits)))) * (1.0 / GLA_TAU))
        small_ref[slot_f, 0, rows_of(ci), :] = _sigmoid(zs)
        small_ref[slot_f, 1, rows_of(ci), :] = -jnp.exp(al_ref[...]) * _softplus(zs + dtb_ref[...])

    pending_front = []
    for ci in chunks:
        pending_front.append(functools.partial(front_small, ci))
        pending_front.extend(functools.partial(front_cols, ci, j) for j in range(ZB_W // GDN_DK))
    ticks = [0]

    def tick():
        ticks[0] += 1
        if pending_front and ticks[0] % FRONT_PERIOD == 0:
            pending_front.pop(0)()

    def drain_front():
        while pending_front:
            pending_front.pop(0)()

    r = lax.broadcasted_iota(jnp.int32, (CHUNK, CHUNK), 0)
    cc = lax.broadcasted_iota(jnp.int32, (CHUNK, CHUNK), 1)
    incl = [cc <= r, cc >= r]
    strict_f = [(cc < r).astype(F32), (cc > r).astype(F32)]
    eye_f = (cc == r).astype(F32)
    blk_shift = INV_BLOCK.bit_length() - 1
    blk_f = (jnp.right_shift(r, blk_shift) == jnp.right_shift(cc, blk_shift)).astype(F32)
    cum_bf = [m.astype(BF16) for m in incl]
    last = [lambda x: x[CHUNK - 1:CHUNK], lambda x: x[0:1]]
    lane = lax.broadcasted_iota(jnp.int32, (1, SMALL_W), 1)
    sk = lambda h: slice(h * GLA_DK, (h + 1) * GLA_DK)

    def back():
        cs = list(chunks)
        cd = [(ci, d) for ci in cs for d in dirs]
        za = {ci: za_ref[rows_of(ci), :] for ci in cs}
        v_bf = {ci: [za[ci][:, 2 * GLA_QK + h * GLA_DV:2 * GLA_QK + (h + 1) * GLA_DV].astype(BF16)
                     for h in range(GLA_HEADS)] for ci in cs}

        def gla_front(ci, d):
            q = za[ci][:, :GLA_QK] * (GLA_DK ** -0.5)
            k = za[ci][:, GLA_QK:2 * GLA_QK]
            bcum = _cum(cum_bf[d], gdec_ref[slot_b, rows_of(ci), d * GLA_QK:(d + 1) * GLA_QK], 2)
            bl = last[d](bcum)
            qe = (q * jnp.exp(bcum)).astype(BF16)
            qe_ref[ci, d] = qe
            er_ref[ci, d] = jnp.exp(bl)
            if exact_gla:
                scores = _gla_scores_exact(q, k, bcum, *exact_scratch)
            else:
                scores = (k * jnp.exp(-bcum)).astype(BF16)
            return qe, scores, (k * jnp.exp(bl - bcum)).astype(BF16)

        qkk = dict(zip(cd, (yield from _stage(gla_front, cd))))
        hd_a = [(ci, d, h) for ci, d in cd for h in range(GLA_HEADS)]

        def masked_scores(ci, d, h):
            qe, scores, _ = qkk[ci, d]
            full = scores[h] if exact_gla else _dg(qe[:, sk(h)], scores[:, sk(h)], _NT)
            return jnp.where(incl[d], full, 0.0).astype(BF16)

        att = yield from _stage(masked_scores, hd_a)
        oi_a = dict(zip(hd_a, (yield from _stage(lambda i: _dg(att[i], v_bf[hd_a[i][0]][hd_a[i][2]]),
                                                  list(range(len(hd_a)))))))

        def gla_c(ci, d, h):
            ca_ref[ci, d, :, h * GLA_DV:(h + 1) * GLA_DV] = _dg(qkk[ci, d][2][:, sk(h)], v_bf[ci][h], _TN).astype(BF16)

        yield from _stage(gla_c, hd_a)

        ch_h = [(ci, h) for ci in cs for h in range(GDN_HEADS)]
        qs = {(ci, h): act_ref[slot_b, rows_of(ci), h * GDN_DK:(h + 1) * GDN_DK] for ci, h in ch_h}
        ks = {(ci, h): act_ref[slot_b, rows_of(ci), GDN_QK + h * GDN_DK:GDN_QK + (h + 1) * GDN_DK] for ci, h in ch_h}
        vs = {(ci, h): act_ref[slot_b, rows_of(ci), 2 * GDN_QK + h * GDN_DV:2 * GDN_QK + (h + 1) * GDN_DV]
              for ci, h in ch_h}
        beta_all = {ci: small_ref[slot_b, 0, rows_of(ci), :] for ci in cs}
        gam_all = dict(zip(cd, (yield from _stage(
            lambda ci, d: _cum(cum_bf[d], small_ref[slot_b, 1, rows_of(ci), :], 3), cd))))
        gam_t = {key: gam_all[key].T for key in cd}
        raw = dict(zip(ch_h, (yield from _stage(
            lambda ci, h: _dg(jnp.concatenate([ks[ci, h], qs[ci, h]], axis=0).astype(BF16),
                              ks[ci, h].astype(BF16), _NT), ch_h))))
        hd_b = [(ci, d, h) for ci, d in cd for h in range(GDN_HEADS)]
        n_b = len(hd_b)

        def decay_terms(ci, d, h):
            ld, lb = LANE_DEC + d * GDN_HEADS + h, LANE_BETA + d * GDN_HEADS + h
            gc = gam_all[ci, d][:, ld:ld + 1]
            gr = gam_t[ci, d][ld:ld + 1, :]
            dmask = jnp.where(incl[d], jnp.exp(jnp.minimum(gc - gr, 0.0)), 0.0)
            return gc, beta_all[ci][:, lb:lb + 1], last[d](gc), dmask

        terms = yield from _stage(decay_terms, hd_b)
        gam_c, beta_c, gl, dm = [list(x) for x in zip(*terms)]
        neg_a = yield from _stage(
            lambda i: raw[hd_b[i][0], hd_b[i][2]][:CHUNK] * (-beta_c[i]) * (dm[i] * strict_f[hd_b[i][1]]),
            list(range(n_b)))
        aqk = [(raw[ci, h][CHUNK:] * dm[i]).astype(BF16) for i, (ci, d, h) in enumerate(hd_b)]
        dbs, naos = yield from _diag_block_inverses(neg_a, eye_f, blk_f)
        tmat = yield from _finish_inverses(dbs, naos, eye_f)
        egc = [jnp.exp(gc) for gc in gam_c]
        uw = yield from _stage(
            lambda i: _dg(tmat[i].astype(BF16), jnp.concatenate(
                [vs[hd_b[i][0], hd_b[i][2]] * beta_c[i], ks[hd_b[i][0], hd_b[i][2]] * (beta_c[i] * egc[i])],
                axis=1).astype(BF16)).astype(BF16), list(range(n_b)))
        aw = yield from _stage(lambda i: _dg(aqk[i], uw[i]), list(range(n_b)))
        pc = yield from _stage(
            lambda i: _dg((ks[hd_b[i][0], hd_b[i][2]] * jnp.exp(gl[i] - gam_c[i])).astype(BF16), uw[i], _TN),
            list(range(n_b)))
        for ci, d in cd:
            row_e = jnp.zeros((1, SMALL_W), F32)
            for h in range(GDN_HEADS):
                row_e = jnp.where(lane == h, jnp.exp(gl[hd_b.index((ci, d, h))]), row_e)
            eg_ref[ci, d] = row_e

        def store_chain(i):
            ci, d, h = hd_b[i]
            lhs_ref[ci, d, h, :GDN_DK] = pc[i][:, GDN_DV:].astype(BF16)
            lhs_ref[ci, d, h, GDN_DK:] = (qs[ci, h] * egc[i] - aw[i][:, GDN_DV:]).astype(BF16)
            cg_ref[ci, d, h] = pc[i][:, :GDN_DV].astype(BF16)

        yield from _stage(store_chain, list(range(n_b)))
        for ci in cs:
            for h in range(GLA_HEADS):
                oi_ref[rows_of(ci), h * GLA_DV:(h + 1) * GLA_DV] = oi_a[ci, 0, h] + oi_a[ci, 1, h]
            for h in range(GDN_HEADS):
                i0, i1 = hd_b.index((ci, 0, h)), hd_b.index((ci, 1, h))
                oi_ref[rows_of(ci), GLA_V + h * GDN_DV:GLA_V + (h + 1) * GDN_DV] = (
                    aw[i0][:, :GDN_DV] + aw[i1][:, :GDN_DV])
            yield

    for _ in back():
        tick()
    drain_front()


def _pre_call(zb, za, zs, cw, up_bd, ub, alv, dtbv, n_lat, ch, exact_gla):
    b, tot, _ = zb.shape
    n_all = tot // CHUNK
    nt_lat, nt_all = n_lat // ch, n_all // ch
    n_tiles = b * nt_all
    tr = ch * CHUNK

    def front_tile(g):
        gf = jnp.minimum(g, n_tiles - 1)
        return lax.div(gf, nt_all), lax.rem(gf, nt_all)

    def back_tile(g):
        gb = jnp.maximum(g - 1, 0)
        return lax.div(gb, nt_all), lax.rem(gb, nt_all)

    def prev_blk(g):
        bi, t = front_tile(g)
        return bi, jnp.where(t >= nt_lat, jnp.maximum(t * ch - 1, n_lat), jnp.maximum(t * ch - 1, 0)), 0

    def next_blk(g):
        bi, t = front_tile(g)
        return bi, jnp.where(t >= nt_lat, jnp.minimum((t + 1) * ch, n_all - 1),
                             jnp.minimum((t + 1) * ch, n_lat - 1)), 0

    kern = functools.partial(_pre_kernel, nt_lat=nt_lat, nt_all=nt_all, n_tiles=n_tiles, ch=ch,
                             exact_gla=exact_gla)
    full = lambda shape: pl.BlockSpec(shape, lambda g: (0,) * len(shape))
    per_tile = lambda *tail: pl.BlockSpec((None, ch) + tail, lambda g: back_tile(g) + (0,) * len(tail))
    return pl.pallas_call(
        kern,
        grid=(n_tiles + 1,),
        in_specs=[
            pl.BlockSpec((None, CHUNK, ZB_W), prev_blk),
            pl.BlockSpec((None, tr, ZB_W), lambda g: front_tile(g) + (0,)),
            pl.BlockSpec((None, CHUNK, ZB_W), next_blk),
            pl.BlockSpec((None, tr, SMALL_W), lambda g: front_tile(g) + (0,)),
            pl.BlockSpec((None, tr, ZA_W), lambda g: back_tile(g) + (0,)),
            full(cw.shape), full(up_bd.shape), full(ub.shape), full(alv.shape), full(dtbv.shape),
        ],
        out_specs=[
            per_tile(N_DIR, GDN_HEADS, GDN_DK + CHUNK, GDN_DV),
            per_tile(N_DIR, GDN_HEADS, GDN_DK, GDN_DV),
            per_tile(N_DIR, 1, SMALL_W),
            per_tile(N_DIR, CHUNK, GLA_QK),
            per_tile(N_DIR, CHUNK, GLA_V),
            per_tile(N_DIR, 1, GLA_QK),
            pl.BlockSpec((None, tr, O_W), lambda g: back_tile(g) + (0,)),
        ],
        out_shape=[
            jax.ShapeDtypeStruct((b, n_all, N_DIR, GDN_HEADS, GDN_DK + CHUNK, GDN_DV), BF16),
            jax.ShapeDtypeStruct((b, n_all, N_DIR, GDN_HEADS, GDN_DK, GDN_DV), BF16),
            jax.ShapeDtypeStruct((b, n_all, N_DIR, 1, SMALL_W), F32),
            jax.ShapeDtypeStruct((b, n_all, N_DIR, CHUNK, GLA_QK), BF16),
            jax.ShapeDtypeStruct((b, n_all, N_DIR, CHUNK, GLA_V), BF16),
            jax.ShapeDtypeStruct((b, n_all, N_DIR, 1, GLA_QK), F32),
            jax.ShapeDtypeStruct((b, tot, O_W), F32),
        ],
        scratch_shapes=[
            pltpu.VMEM((2, tr, ZB_W), F32),
            pltpu.VMEM((2, tr, N_DIR * GLA_QK), F32),
            pltpu.VMEM((2, 2, tr, SMALL_W), F32),
        ] + ([pltpu.VMEM((CHUNK, GLA_QK), F32)] * 2 if exact_gla else []),
        compiler_params=pltpu.CompilerParams(dimension_semantics=("arbitrary",), vmem_limit_bytes=V7X_VMEM_LIMIT),
        name="chunk_pre",
    )(zb, zb, zb, zs, za, cw, up_bd, ub, alv, dtbv)


def _scan_kernel(*refs, ch):
    ins, (of_ref, ob_ref, sa_ref, sb_ref) = refs[:12], refs[12:]
    per_dir = [ins[0:6], ins[6:12]]
    o_refs = [of_ref, ob_ref]
    j = pl.program_id(1)

    @pl.when(j == 0)
    def _():
        sa_ref[...] = jnp.zeros_like(sa_ref)
        sb_ref[...] = jnp.zeros_like(sb_ref)

    r = lax.broadcasted_iota(jnp.int32, (GLA_DK, GLA_DK), 0)
    cc = lax.broadcasted_iota(jnp.int32, (GLA_DK, GLA_DK), 1)
    eye = cc == r

    hd = [(d, h) for d in range(N_DIR) for h in range(GDN_HEADS)]
    s_b = [sb_ref[d, h] for d, h in hd]
    s_a = [sa_ref[d, h] for d, h in hd]
    for s in range(ch):
        ci = [s, ch - 1 - s]
        rows = [slice(c * CHUNK, (c + 1) * CHUNK) for c in ci]
        res_b = [_dg(per_dir[d][0][ci[d], h], s_b[i].astype(BF16)) for i, (d, h) in enumerate(hd)]
        for i, (d, h) in enumerate(hd):
            e = per_dir[d][2][ci[d], :, h:h + 1]
            s_b[i] = s_b[i] * e - res_b[i][:GDN_DK] + per_dir[d][1][ci[d], h].astype(F32)
            o_refs[d][rows[d], GLA_V + h * GDN_DV:GLA_V + (h + 1) * GDN_DV] = res_b[i][GDN_DK:].astype(BF16)
        res_a = [_dg(per_dir[d][3][ci[d], :, h * GLA_DK:(h + 1) * GLA_DK], s_a[i].astype(BF16))
                 for i, (d, h) in enumerate(hd)]
        for i, (d, h) in enumerate(hd):
            e_row = per_dir[d][5][ci[d], :, h * GLA_DK:(h + 1) * GLA_DK]
            ecol = jnp.sum(jnp.where(eye, e_row, 0.0), axis=1, keepdims=True)
            s_a[i] = s_a[i] * ecol + per_dir[d][4][ci[d], :, h * GLA_DV:(h + 1) * GLA_DV].astype(F32)
            o_refs[d][rows[d], h * GLA_DV:(h + 1) * GLA_DV] = res_a[i].astype(BF16)
    for i, (d, h) in enumerate(hd):
        sb_ref[d, h] = s_b[i]
        sa_ref[d, h] = s_a[i]


def _scan_call(lhs, cg, eg, qe, ca, er, n_lat, ch):
    b, n_all = lhs.shape[0], lhs.shape[1]
    nt_all, nt_lat = n_all // ch, n_lat // ch
    nt_ctx = nt_all - nt_lat
    tr = ch * CHUNK

    def in_blk(dd, j):
        if dd == 0:
            return jnp.where(j < nt_ctx, nt_lat + j, j - nt_ctx)
        return nt_all - 1 - j

    def out_blk(dd, j):
        jj = jnp.maximum(j, nt_ctx)
        return jj - nt_ctx if dd == 0 else nt_all - 1 - jj

    def spec(arr, dd):
        tail = arr.shape[3:]
        return pl.BlockSpec((None, ch, None) + tail,
                            lambda bi, j: (bi, in_blk(dd, j), dd) + (0,) * len(tail))

    arrs = (lhs, cg, eg, qe, ca, er)
    in_specs = [spec(a, dd) for dd in range(N_DIR) for a in arrs]
    out_spec = lambda dd: pl.BlockSpec((None, tr, O_W), lambda bi, j: (bi, out_blk(dd, j), 0))
    o_shape = jax.ShapeDtypeStruct((b, n_lat * CHUNK, O_W), BF16)
    return pl.pallas_call(
        functools.partial(_scan_kernel, ch=ch),
        grid=(b, nt_all),
        in_specs=in_specs,
        out_specs=[out_spec(0), out_spec(1)],
        out_shape=[o_shape, o_shape],
        scratch_shapes=[
            pltpu.VMEM((N_DIR, GLA_HEADS, GLA_DK, GLA_DV), F32),
            pltpu.VMEM((N_DIR, GDN_HEADS, GDN_DK, GDN_DV), F32),
        ],
        compiler_params=pltpu.CompilerParams(
            dimension_semantics=("arbitrary", "arbitrary"), vmem_limit_bytes=V7X_VMEM_LIMIT),
        name="state_scan",
    )(*(arrs + arrs))


def _merge_kernel(oi_ref, of_ref, ob_ref, zg_ref, x_ref, mod_ref, wa_ref, wb_ref, wo_ref, ga_ref, gb_ref,
                  bg_ref, fg_ref, y_ref, *, d_model):
    o = oi_ref[...] + of_ref[...].astype(F32) + ob_ref[...].astype(F32)
    zg = zg_ref[...].astype(F32)

    def head_norm(xh, gain):
        ms = jnp.mean(xh * xh, axis=-1, keepdims=True)
        return xh * lax.rsqrt(ms + EPS) * gain

    ya = jnp.concatenate(
        [head_norm(o[:, h * GLA_DV:(h + 1) * GLA_DV], ga_ref[...]) for h in range(GLA_HEADS)], axis=1)
    yb = jnp.concatenate(
        [head_norm(o[:, GLA_V + h * GDN_DV:GLA_V + (h + 1) * GDN_DV], gb_ref[...]) for h in range(GDN_HEADS)],
        axis=1)
    ya = ya * _silu(zg[:, :GLA_V])
    yb = yb * _silu(zg[:, GLA_V:GLA_V + GDN_V])
    pa = _dg(ya.astype(BF16), wa_ref[...])
    pb = _dg(yb.astype(BF16), wb_ref[...])
    gates = _sigmoid(zg[:, ZG_W:] + bg_ref[...])
    merged = gates[:, :d_model] * pa + gates[:, d_model:] * pb
    out = _dg(merged.astype(BF16), wo_ref[...])
    gate = mod_ref[...][:, 2 * d_model:]
    xn = x_ref[...] + gate * out
    ms = jnp.mean(xn * xn, axis=-1, keepdims=True)
    y_ref[...] = xn * lax.rsqrt(ms + EPS) * fg_ref[...]


def _merge_call(oi, of, ob, zg, x, mod3, wa, wb, wo, ga, gb, bg, fg):
    b, seq, d = x.shape
    tm = 256
    kern = functools.partial(_merge_kernel, d_model=d)
    full = lambda a: pl.BlockSpec(a.shape, lambda bi, t: (0,) * a.ndim)
    rows = lambda width: pl.BlockSpec((None, tm, width), lambda bi, t: (bi, t, 0))
    return pl.pallas_call(
        kern,
        grid=(b, seq // tm),
        in_specs=[
            rows(O_W), rows(O_W), rows(O_W), rows(zg.shape[2]), rows(d),
            pl.BlockSpec((None, 1, 3 * d), lambda bi, t: (bi, 0, 0)),
            full(wa), full(wb), full(wo), full(ga), full(gb), full(bg), full(fg),
        ],
        out_specs=rows(d),
        out_shape=jax.ShapeDtypeStruct((b, seq, d), x.dtype),
        compiler_params=pltpu.CompilerParams(
            dimension_semantics=("arbitrary", "arbitrary"), vmem_limit_bytes=V7X_VMEM_LIMIT),
        name="merge_out",
    )(oi, of, ob, zg, x, mod3, wa, wb, wo, ga, gb, bg, fg)


def _regroup_w_in(w):
    d = w.shape[0]
    o = 0
    offs = []
    for size in (GLA_QK, GLA_QK, GLA_V, GLA_V, N_DIR * GLA_RANK, GDN_QK, GDN_QK, GDN_V, GDN_V,
                 N_DIR * GDN_HEADS, N_DIR * GDN_HEADS, 2 * d):
        offs.append((o, o + size))
        o += size
    aq, ak, av, ag, alr, bq, bk, bv, bg, bbeta, bdec, mg = [w[:, s:e] for s, e in offs]
    small = jnp.concatenate([alr, bbeta, bdec], axis=1)
    small = jnp.pad(small, ((0, 0), (0, SMALL_W - small.shape[1])))
    return jnp.concatenate([aq, ak, av, bq, bk, bv, ag, bg, mg, small], axis=1).astype(BF16)


def kernel(x, c, ctx, c_ctx, w_mod, b_mod, norm_g, w_in, gla_up, gla_ub, gla_onorm, gdn_conv, gdn_a_log,
           gdn_dt_bias, gdn_onorm, w_gla_out, w_gdn_out, b_gate, w_o, final_g):
    b, seq, d = x.shape
    assert w_mod.shape[0] == 1, "single-layer block"
    assert seq % CHUNK == 0 and ctx.shape[1] % CHUNK == 0 and seq % ctx.shape[1] == 0
    n_lat = seq // CHUNK
    ch = _tile_chunks(n_lat, ctx.shape[1] // CHUNK)

    rows = -(-(b + 1) // 8) * 8
    cc = jnp.concatenate([c, c_ctx[None, :], jnp.zeros((rows - b - 1, d), F32)], axis=0)
    mod = _mod_call(cc, w_mod[0], b_mod[0][None, :])
    mod3 = mod[:, None, :]

    w_r = _regroup_w_in(w_in[0])
    za, zb, zg, zs, nrm = _proj_call(x, ctx, mod3, norm_g[0][None, :], w_r)

    cw = gdn_conv[0].reshape(9, CONV_CH)
    up_bd = jnp.concatenate(
        [jnp.pad(gla_up[0, n], ((0, 0), (n * GLA_QK, (N_DIR - 1 - n) * GLA_QK))) for n in range(N_DIR)], axis=0)
    up_bd = jnp.pad(up_bd, ((0, SMALL_W - N_DIR * GLA_RANK), (0, 0)))
    ub = gla_ub[0].reshape(1, N_DIR * GLA_QK)
    nd = N_DIR * GDN_HEADS
    lane_pad = ((0, 0), (LANE_DEC, SMALL_W - LANE_DEC - nd))
    alv = jnp.pad(gdn_a_log[0].reshape(1, nd), lane_pad)
    dtbv = jnp.pad(gdn_dt_bias[0].reshape(1, nd), lane_pad)
    wa, wb, wo = w_gla_out[0].astype(BF16), w_gdn_out[0].astype(BF16), w_o[0].astype(BF16)

    def mixers_and_merge(exact_gla):
        def run(za, zb, zg, zs):
            lhs, cg, eg, qe, ca, er, oi = _pre_call(zb, za, zs, cw, up_bd, ub, alv, dtbv, n_lat, ch, exact_gla)
            of, ob = _scan_call(lhs, cg, eg, qe, ca, er, n_lat, ch)
            return _merge_call(oi, of, ob, zg, x, mod3, wa, wb, wo, gla_onorm[0][None, :], gdn_onorm[0][None, :],
                               b_gate[0][None, :], final_g[None, :])
        return run

    alr_norm = jnp.sqrt(jnp.max(nrm[:, :, 0, :N_DIR], axis=(0, 1)))
    up_norm = jnp.sqrt(jnp.max(jnp.sum(gla_up[0] * gla_up[0], axis=1), axis=-1))
    decay_bound = CHUNK * (alr_norm * up_norm + jnp.max(jnp.abs(gla_ub[0]), axis=-1) + jnp.log(2.0)) / GLA_TAU
    return lax.cond(jnp.max(decay_bound) > GLA_FACTOR_LIMIT, mixers_and_merge(True), mixers_and_merge(False),
                    za, zb, zg, zs)
```

```python
import functools

import jax
import jax.numpy as jnp
from jax import lax
from jax.experimental import pallas as pl
from jax.experimental.pallas import tpu as pltpu

F32 = jnp.float32
BF16 = jnp.bfloat16

CHUNK = 64
N_DIR = 2
GLA_HEADS, GLA_DK, GLA_DV, GLA_RANK, GLA_TAU = 4, 64, 128, 16, 16.0
GLA_QK, GLA_V = GLA_HEADS * GLA_DK, GLA_HEADS * GLA_DV
GDN_HEADS, GDN_DK, GDN_DV = 4, 128, 128
GDN_QK, GDN_V = GDN_HEADS * GDN_DK, GDN_HEADS * GDN_DV
CONV_CH = 2 * GDN_QK + GDN_V
EPS = 1e-6
GLA_FACTOR_LIMIT = 80.0
O_W = GLA_V + GDN_V

LANE_ALR = 0
LANE_BETA = N_DIR * GLA_RANK
LANE_DEC = LANE_BETA + N_DIR * GDN_HEADS
SMALL_W = 128

ZA_W = 2 * GLA_QK + GLA_V
ZB_W = CONV_CH
ZG_W = GLA_V + GDN_V

V7X_VMEM_LIMIT = 56 * 1024 * 1024

_NN = (((1,), (0,)), ((), ()))
_NT = (((1,), (1,)), ((), ()))
_TN = (((0,), (0,)), ((), ()))


def _dg(a, b, dims=_NN):
    return lax.dot_general(a, b, dims, preferred_element_type=F32)


def _dot3(a, b, dims=_NN):
    ah = a.astype(BF16)
    al = (a - ah.astype(F32)).astype(BF16)
    bh = b.astype(BF16)
    bl = (b - bh.astype(F32)).astype(BF16)
    return _dg(ah, bh, dims) + _dg(ah, bl, dims) + _dg(al, bh, dims)


def _cum(mask_bf, x, passes):
    acc = None
    rem = x
    for _ in range(passes):
        piece = rem.astype(BF16)
        term = _dg(mask_bf, piece)
        acc = term if acc is None else acc + term
        rem = rem - piece.astype(F32)
    return acc


def _sigmoid(x):
    return 1.0 / (1.0 + jnp.exp(-x))


def _silu(x):
    return x * _sigmoid(x)


def _softplus(x):
    return jnp.maximum(x, 0.0) + jnp.log1p(jnp.exp(-jnp.abs(x)))


def _tile_chunks(n_lat, n_ctx):
    for ch in (4, 2, 1):
        if n_lat % ch == 0 and n_ctx % ch == 0:
            return ch


def _mod_kernel(c_ref, w_ref, b_ref, o_ref):
    o_ref[...] = _dot3(_silu(c_ref[...]), w_ref[...]) + b_ref[...]


def _mod_call(cc, w_mod, b_mod):
    rows, d = cc.shape
    n = w_mod.shape[1]
    bn = d
    return pl.pallas_call(
        _mod_kernel,
        grid=(n // bn,),
        in_specs=[
            pl.BlockSpec((rows, d), lambda i: (0, 0)),
            pl.BlockSpec((d, bn), lambda i: (0, i)),
            pl.BlockSpec((1, bn), lambda i: (0, i)),
        ],
        out_specs=pl.BlockSpec((rows, bn), lambda i: (0, i)),
        out_shape=jax.ShapeDtypeStruct((rows, n), F32),
        compiler_params=pltpu.CompilerParams(dimension_semantics=("arbitrary",), vmem_limit_bytes=V7X_VMEM_LIMIT),
        name="adaln_mod",
    )(cc, w_mod, b_mod)


def _proj_kernel(x_ref, ctx_ref, mod_ref, g_ref, w_ref, za_ref, zb_ref, zg_ref, zs_ref, nrm_ref, *, d_model):
    t = pl.program_id(1)
    xin = jnp.where(t == 0, ctx_ref[...], x_ref[...])
    ms = jnp.mean(xin * xin, axis=-1, keepdims=True)
    y = xin * lax.rsqrt(ms + EPS) * g_ref[...]
    mod = mod_ref[...]
    shift = mod[:, :d_model]
    scale = mod[:, d_model:2 * d_model]
    h = (y * (1.0 + scale) + shift).astype(BF16)
    o0, o1, o2, o3 = 0, ZA_W, ZA_W + ZB_W, ZA_W + ZB_W + ZG_W + 2 * d_model
    zs = _dg(h, w_ref[:, o3:o3 + SMALL_W])
    zs_ref[...] = zs
    lane = lax.broadcasted_iota(jnp.int32, (1, SMALL_W), 1)
    sq = zs * zs
    nrm = jnp.zeros((1, SMALL_W), F32)
    for n in range(N_DIR):
        in_dir = (lane >= LANE_ALR + n * GLA_RANK) & (lane < LANE_ALR + (n + 1) * GLA_RANK)
        worst = jnp.max(jnp.sum(jnp.where(in_dir, sq, 0.0), axis=1, keepdims=True), axis=0, keepdims=True)
        nrm = jnp.where(lane == n, worst, nrm)
    nrm_ref[...] = jnp.broadcast_to(nrm, nrm_ref.shape)
    za_ref[...] = _dg(h, w_ref[:, o0:o1])
    zb_ref[...] = _dg(h, w_ref[:, o1:o2])

    @pl.when(t > 0)
    def _():
        zg_ref[...] = _dg(h, w_ref[:, o2:o3]).astype(zg_ref.dtype)


def _proj_call(x, ctx, mod3, norm_g, w_r):
    b, seq, d = x.shape
    ctx_len = ctx.shape[1]
    tm = ctx_len
    n_lat = seq // tm
    tot = seq + ctx_len
    zgw = ZG_W + 2 * d
    n_all = w_r.shape[1]

    def row_blk(bi, t):
        return jnp.where(t == 0, n_lat, t - 1)

    kern = functools.partial(_proj_kernel, d_model=d)
    return pl.pallas_call(
        kern,
        grid=(b, n_lat + 1),
        in_specs=[
            pl.BlockSpec((None, tm, d), lambda bi, t: (bi, jnp.maximum(t - 1, 0), 0)),
            pl.BlockSpec((None, tm, d), lambda bi, t: (bi, 0, 0)),
            pl.BlockSpec((None, 1, 3 * d), lambda bi, t: (jnp.where(t == 0, b, bi), 0, 0)),
            pl.BlockSpec((1, d), lambda bi, t: (0, 0)),
            pl.BlockSpec((d, n_all), lambda bi, t: (0, 0), pipeline_mode=pl.Buffered(1)),
        ],
        out_specs=[
            pl.BlockSpec((None, tm, ZA_W), lambda bi, t: (bi, row_blk(bi, t), 0)),
            pl.BlockSpec((None, tm, ZB_W), lambda bi, t: (bi, row_blk(bi, t), 0)),
            pl.BlockSpec((None, tm, zgw), lambda bi, t: (bi, jnp.maximum(t - 1, 0), 0)),
            pl.BlockSpec((None, tm, SMALL_W), lambda bi, t: (bi, row_blk(bi, t), 0)),
            pl.BlockSpec((None, None, 8, SMALL_W), lambda bi, t: (bi, t, 0, 0)),
        ],
        out_shape=[
            jax.ShapeDtypeStruct((b, tot, ZA_W), F32),
            jax.ShapeDtypeStruct((b, tot, ZB_W), F32),
            jax.ShapeDtypeStruct((b, seq, zgw), BF16),
            jax.ShapeDtypeStruct((b, tot, SMALL_W), F32),
            jax.ShapeDtypeStruct((b, n_lat + 1, 8, SMALL_W), F32),
        ],
        compiler_params=pltpu.CompilerParams(
            dimension_semantics=("arbitrary", "arbitrary"), vmem_limit_bytes=V7X_VMEM_LIMIT),
        name="in_proj",
    )(x, ctx, mod3, norm_g, w_r)


INV_BLOCK = 16
FRONT_PERIOD = 14


def _stage(fn, items):
    out = []
    for it in items:
        out.append(fn(*it) if isinstance(it, tuple) else fn(it))
        yield
    return out


def _diag_block_inverses(neg_a_list, eye_f, blk_f):
    nds = [na * blk_f for na in neg_a_list]
    naos = [(na - nd).astype(BF16) for na, nd in zip(neg_a_list, nds)]
    ds = [eye_f + nd for nd in nds]
    ns = nds
    for _ in range(INV_BLOCK.bit_length() - 2):
        ns = yield from _stage(lambda n: _dg(n.astype(BF16), n.astype(BF16)), ns)
        ds = yield from _stage(lambda d, n: d + _dg(d.astype(BF16), n.astype(BF16)), list(zip(ds, ns)))
    return [d.astype(BF16) for d in ds], naos


def _finish_inverses(dbs, naos, eye_f):
    ms = yield from _stage(_dg, list(zip(dbs, naos)))
    mbs = [m.astype(BF16) for m in ms]
    m2s = yield from _stage(lambda mb: _dg(mb, mb), mbs)
    m3s = yield from _stage(lambda mb, m2: _dg(mb, m2.astype(BF16)), list(zip(mbs, m2s)))
    ss = [(eye_f + m + m2 + m3).astype(BF16) for m, m2, m3 in zip(ms, m2s, m3s)]
    return (yield from _stage(_dg, list(zip(ss, dbs))))


def _gla_scores_exact(q, k, bcum, k_ref, b_ref):
    k_ref[...] = k
    b_ref[...] = bcum
    lane_s = lax.broadcasted_iota(jnp.int32, (1, CHUNK), 1)
    rr = lax.broadcasted_iota(jnp.int32, (GLA_QK, SMALL_W), 0)
    ll = lax.broadcasted_iota(jnp.int32, (GLA_QK, SMALL_W), 1)
    head_sum = (jnp.right_shift(rr, GLA_DK.bit_length() - 1) == ll).astype(F32)

    def column(s, acc):
        p = q * jnp.exp(jnp.minimum(bcum - b_ref[pl.ds(s, 1), :], 0.0)) * k_ref[pl.ds(s, 1), :]
        per_head = _dot3(p, head_sum)
        hit = (lane_s == s).astype(F32)
        return tuple(a + per_head[:, h:h + 1] * hit for h, a in enumerate(acc))

    zero = jnp.zeros((CHUNK, CHUNK), F32)
    return lax.fori_loop(0, CHUNK, column, (zero,) * GLA_HEADS)


def _pre_kernel(zp_ref, zc_ref, zn_ref, zs_ref, za_ref, cw_ref, up_ref, ub_ref, al_ref, dtb_ref,
                lhs_ref, cg_ref, eg_ref, qe_ref, ca_ref, er_ref, oi_ref, act_ref, gdec_ref, small_ref, *exact_scratch,
                nt_lat, nt_all, n_tiles, ch, exact_gla):
    g = pl.program_id(0)
    tf = lax.rem(jnp.minimum(g, n_tiles - 1), nt_all)
    slot_f = lax.rem(g, 2)
    slot_b = 1 - slot_f
    rows_of = lambda ci: slice(ci * CHUNK, (ci + 1) * CHUNK)
    dirs = range(N_DIR)
    chunks = range(ch)

    @pl.when(g == 0)
    def _():
        act_ref[1] = jnp.zeros(act_ref.shape[1:], F32)
        gdec_ref[1] = jnp.zeros(gdec_ref.shape[1:], F32)
        small_ref[1] = jnp.zeros(small_ref.shape[1:], F32)

    f_ctx = tf >= nt_lat
    row = lax.broadcasted_iota(jnp.int32, (CHUNK, 1), 0)
    w = cw_ref[...]
    lat_f = jnp.logical_not(f_ctx).astype(F32)
    ctx_f = f_ctx.astype(F32)
    pv = (tf > jnp.where(f_ctx, nt_lat, 0)).astype(F32)
    nv = (tf < jnp.where(f_ctx, nt_all - 1, nt_lat - 1)).astype(F32)
    w_up, w_mid, w_dn = w[0:3] * lat_f, w[3:6], w[6:9] * lat_f
    w_edge_l, w_edge_r = w[3:4] * ctx_f, w[5:6] * ctx_f

    def front_cols(ci, j):
        cols = slice(j * GDN_DK, (j + 1) * GDN_DK)
        prev = zp_ref[:, cols] * pv if ci == 0 else zc_ref[rows_of(ci - 1), cols]
        nxt = zn_ref[:, cols] * nv if ci == ch - 1 else zc_ref[rows_of(ci + 1), cols]
        cur = zc_ref[rows_of(ci), cols]

        def col(dc):
            return (w_up[dc:dc + 1, cols] * prev + w_mid[dc:dc + 1, cols] * cur + w_dn[dc:dc + 1, cols] * nxt)

        left = jnp.where(row >= 1, pltpu.roll(col(0), 1, 0), w_edge_l[:, cols] * prev[CHUNK - 1:CHUNK])
        right = jnp.where(row <= CHUNK - 2, pltpu.roll(col(2), CHUNK - 1, 0), w_edge_r[:, cols] * nxt[0:1])
        a = _silu(col(1) + left + right)
        if j < 2 * GDN_HEADS:
            inv = lax.rsqrt(jnp.sum(a * a, axis=-1, keepdims=True) + EPS)
            a = a * (inv * (GDN_DK ** -0.5 if j < GDN_HEADS else 1.0))
        act_ref[slot_f, rows_of(ci), cols] = a

    def front_small(ci):
        zs = zs_ref[rows_of(ci), :]
        logits = _dot3(zs, up_ref[...]) + ub_ref[...]
        gdec_ref[slot_f, rows_of(ci), :] = (
            (jnp.minimum(logits, 0.0) - jnp.log1p(jnp.exp(-jnp.abs(logits)))) * (1.0 / GLA_TAU))
        small_ref[slot_f, 0, rows_of(ci), :] = _sigmoid(zs)
        small_ref[slot_f, 1, rows_of(ci), :] = -jnp.exp(al_ref[...]) * _softplus(zs + dtb_ref[...])

    pending_front = []
    for ci in chunks:
        pending_front.append(functools.partial(front_small, ci))
        pending_front.extend(functools.partial(front_cols, ci, j) for j in range(ZB_W // GDN_DK))
    ticks = [0]

    def tick():
        ticks[0] += 1
        if pending_front and ticks[0] % FRONT_PERIOD == 0:
            pending_front.pop(0)()

    def drain_front():
        while pending_front:
            pending_front.pop(0)()

    r = lax.broadcasted_iota(jnp.int32, (CHUNK, CHUNK), 0)
    cc = lax.broadcasted_iota(jnp.int32, (CHUNK, CHUNK), 1)
    incl = [cc <= r, cc >= r]
    strict_f = [(cc < r).astype(F32), (cc > r).astype(F32)]
    eye_f = (cc == r).astype(F32)
    blk_shift = INV_BLOCK.bit_length() - 1
    blk_f = (jnp.right_shift(r, blk_shift) == jnp.right_shift(cc, blk_shift)).astype(F32)
    cum_bf = [m.astype(BF16) for m in incl]
    last = [lambda x: x[CHUNK - 1:CHUNK], lambda x: x[0:1]]
    lane = lax.broadcasted_iota(jnp.int32, (1, SMALL_W), 1)
    sk = lambda h: slice(h * GLA_DK, (h + 1) * GLA_DK)

    def back():
        cs = list(chunks)
        cd = [(ci, d) for ci in cs for d in dirs]
        za = {ci: za_ref[rows_of(ci), :] for ci in cs}
        v_bf = {ci: [za[ci][:, 2 * GLA_QK + h * GLA_DV:2 * GLA_QK + (h + 1) * GLA_DV].astype(BF16)
                     for h in range(GLA_HEADS)] for ci in cs}

        def gla_front(ci, d):
            q = za[ci][:, :GLA_QK] * (GLA_DK ** -0.5)
            k = za[ci][:, GLA_QK:2 * GLA_QK]
            bcum = _cum(cum_bf[d], gdec_ref[slot_b, rows_of(ci), d * GLA_QK:(d + 1) * GLA_QK], 2)
            bl = last[d](bcum)
            qe = (q * jnp.exp(bcum)).astype(BF16)
            qe_ref[ci, d] = qe
            er_ref[ci, d] = jnp.exp(bl)
            if exact_gla:
                scores = _gla_scores_exact(q, k, bcum, *exact_scratch)
            else:
                scores = (k * jnp.exp(-bcum)).astype(BF16)
            return qe, scores, (k * jnp.exp(bl - bcum)).astype(BF16)

        qkk = dict(zip(cd, (yield from _stage(gla_front, cd))))
        hd_a = [(ci, d, h) for ci, d in cd for h in range(GLA_HEADS)]

        def masked_scores(ci, d, h):
            qe, scores, _ = qkk[ci, d]
            full = scores[h] if exact_gla else _dg(qe[:, sk(h)], scores[:, sk(h)], _NT)
            return jnp.where(incl[d], full, 0.0).astype(BF16)

        att = yield from _stage(masked_scores, hd_a)
        oi_a = dict(zip(hd_a, (yield from _stage(lambda i: _dg(att[i], v_bf[hd_a[i][0]][hd_a[i][2]]),
                                                  list(range(len(hd_a)))))))

        def gla_c(ci, d, h):
            ca_ref[ci, d, :, h * GLA_DV:(h + 1) * GLA_DV] = _dg(qkk[ci, d][2][:, sk(h)], v_bf[ci][h], _TN).astype(BF16)

        yield from _stage(gla_c, hd_a)

        ch_h = [(ci, h) for ci in cs for h in range(GDN_HEADS)]
        qs = {(ci, h): act_ref[slot_b, rows_of(ci), h * GDN_DK:(h + 1) * GDN_DK] for ci, h in ch_h}
        ks = {(ci, h): act_ref[slot_b, rows_of(ci), GDN_QK + h * GDN_DK:GDN_QK + (h + 1) * GDN_DK] for ci, h in ch_h}
        vs = {(ci, h): act_ref[slot_b, rows_of(ci), 2 * GDN_QK + h * GDN_DV:2 * GDN_QK + (h + 1) * GDN_DV]
              for ci, h in ch_h}
        beta_all = {ci: small_ref[slot_b, 0, rows_of(ci), :] for ci in cs}
        gam_all = dict(zip(cd, (yield from _stage(
            lambda ci, d: _cum(cum_bf[d], small_ref[slot_b, 1, rows_of(ci), :], 3), cd))))
        gam_t = {key: gam_all[key].T for key in cd}
        raw = dict(zip(ch_h, (yield from _stage(
            lambda ci, h: _dg(jnp.concatenate([ks[ci, h], qs[ci, h]], axis=0).astype(BF16),
                              ks[ci, h].astype(BF16), _NT), ch_h))))
        hd_b = [(ci, d, h) for ci, d in cd for h in range(GDN_HEADS)]
        n_b = len(hd_b)

        def decay_terms(ci, d, h):
            ld, lb = LANE_DEC + d * GDN_HEADS + h, LANE_BETA + d * GDN_HEADS + h
            gc = gam_all[ci, d][:, ld:ld + 1]
            gr = gam_t[ci, d][ld:ld + 1, :]
            dmask = jnp.where(incl[d], jnp.exp(jnp.minimum(gc - gr, 0.0)), 0.0)
            return gc, beta_all[ci][:, lb:lb + 1], last[d](gc), dmask

        terms = yield from _stage(decay_terms, hd_b)
        gam_c, beta_c, gl, dm = [list(x) for x in zip(*terms)]
        neg_a = yield from _stage(
            lambda i: raw[hd_b[i][0], hd_b[i][2]][:CHUNK] * (-beta_c[i]) * (dm[i] * strict_f[hd_b[i][1]]),
            list(range(n_b)))
        aqk = [(raw[ci, h][CHUNK:] * dm[i]).astype(BF16) for i, (ci, d, h) in enumerate(hd_b)]
        dbs, naos = yield from _diag_block_inverses(neg_a, eye_f, blk_f)
        tmat = yield from _finish_inverses(dbs, naos, eye_f)
        egc = [jnp.exp(gc) for gc in gam_c]
        uw = yield from _stage(
            lambda i: _dg(tmat[i].astype(BF16), jnp.concatenate(
                [vs[hd_b[i][0], hd_b[i][2]] * beta_c[i], ks[hd_b[i][0], hd_b[i][2]] * (beta_c[i] * egc[i])],
                axis=1).astype(BF16)).astype(BF16), list(range(n_b)))
        aw = yield from _stage(lambda i: _dg(aqk[i], uw[i]), list(range(n_b)))
        pc = yield from _stage(
            lambda i: _dg((ks[hd_b[i][0], hd_b[i][2]] * jnp.exp(gl[i] - gam_c[i])).astype(BF16), uw[i], _TN),
            list(range(n_b)))
        for ci, d in cd:
            row_e = jnp.zeros((1, SMALL_W), F32)
            for h in range(GDN_HEADS):
                row_e = jnp.where(lane == h, jnp.exp(gl[hd_b.index((ci, d, h))]), row_e)
            eg_ref[ci, d] = row_e

        def store_chain(i):
            ci, d, h = hd_b[i]
            lhs_ref[ci, d, h, :GDN_DK] = pc[i][:, GDN_DV:].astype(BF16)
            lhs_ref[ci, d, h, GDN_DK:] = (qs[ci, h] * egc[i] - aw[i][:, GDN_DV:]).astype(BF16)
            cg_ref[ci, d, h] = pc[i][:, :GDN_DV].astype(BF16)

        yield from _stage(store_chain, list(range(n_b)))
        for ci in cs:
            for h in range(GLA_HEADS):
                oi_ref[rows_of(ci), h * GLA_DV:(h + 1) * GLA_DV] = oi_a[ci, 0, h] + oi_a[ci, 1, h]
            for h in range(GDN_HEADS):
                i0, i1 = hd_b.index((ci, 0, h)), hd_b.index((ci, 1, h))
                oi_ref[rows_of(ci), GLA_V + h * GDN_DV:GLA_V + (h + 1) * GDN_DV] = (
                    aw[i0][:, :GDN_DV] + aw[i1][:, :GDN_DV])
            yield

    for _ in back():
        tick()
    drain_front()


def _pre_call(zb, za, zs, cw, up_bd, ub, alv, dtbv, n_lat, ch, exact_gla):
    b, tot, _ = zb.shape
    n_all = tot // CHUNK
    nt_lat, nt_all = n_lat // ch, n_all // ch
    n_tiles = b * nt_all
    tr = ch * CHUNK

    def front_tile(g):
        gf = jnp.minimum(g, n_tiles - 1)
        return lax.div(gf, nt_all), lax.rem(gf, nt_all)

    def back_tile(g):
        gb = jnp.maximum(g - 1, 0)
        return lax.div(gb, nt_all), lax.rem(gb, nt_all)

    def prev_blk(g):
        bi, t = front_tile(g)
        return bi, jnp.where(t >= nt_lat, jnp.maximum(t * ch - 1, n_lat), jnp.maximum(t * ch - 1, 0)), 0

    def next_blk(g):
        bi, t = front_tile(g)
        return bi, jnp.where(t >= nt_lat, jnp.minimum((t + 1) * ch, n_all - 1),
                             jnp.minimum((t + 1) * ch, n_lat - 1)), 0

    kern = functools.partial(_pre_kernel, nt_lat=nt_lat, nt_all=nt_all, n_tiles=n_tiles, ch=ch,
                             exact_gla=exact_gla)
    full = lambda shape: pl.BlockSpec(shape, lambda g: (0,) * len(shape))
    per_tile = lambda *tail: pl.BlockSpec((None, ch) + tail, lambda g: back_tile(g) + (0,) * len(tail))
    return pl.pallas_call(
        kern,
        grid=(n_tiles + 1,),
        in_specs=[
            pl.BlockSpec((None, CHUNK, ZB_W), prev_blk),
            pl.BlockSpec((None, tr, ZB_W), lambda g: front_tile(g) + (0,)),
            pl.BlockSpec((None, CHUNK, ZB_W), next_blk),
            pl.BlockSpec((None, tr, SMALL_W), lambda g: front_tile(g) + (0,)),
            pl.BlockSpec((None, tr, ZA_W), lambda g: back_tile(g) + (0,)),
            full(cw.shape), full(up_bd.shape), full(ub.shape), full(alv.shape), full(dtbv.shape),
        ],
        out_specs=[
            per_tile(N_DIR, GDN_HEADS, GDN_DK + CHUNK, GDN_DV),
            per_tile(N_DIR, GDN_HEADS, GDN_DK, GDN_DV),
            per_tile(N_DIR, 1, SMALL_W),
            per_tile(N_DIR, CHUNK, GLA_QK),
            per_tile(N_DIR, CHUNK, GLA_V),
            per_tile(N_DIR, 1, GLA_QK),
            pl.BlockSpec((None, tr, O_W), lambda g: back_tile(g) + (0,)),
        ],
        out_shape=[
            jax.ShapeDtypeStruct((b, n_all, N_DIR, GDN_HEADS, GDN_DK + CHUNK, GDN_DV), BF16),
            jax.ShapeDtypeStruct((b, n_all, N_DIR, GDN_HEADS, GDN_DK, GDN_DV), BF16),
            jax.ShapeDtypeStruct((b, n_all, N_DIR, 1, SMALL_W), F32),
            jax.ShapeDtypeStruct((b, n_all, N_DIR, CHUNK, GLA_QK), BF16),
            jax.ShapeDtypeStruct((b, n_all, N_DIR, CHUNK, GLA_V), BF16),
            jax.ShapeDtypeStruct((b, n_all, N_DIR, 1, GLA_QK), F32),
            jax.ShapeDtypeStruct((b, tot, O_W), F32),
        ],
        scratch_shapes=[
            pltpu.VMEM((2, tr, ZB_W), F32),
            pltpu.VMEM((2, tr, N_DIR * GLA_QK), F32),
            pltpu.VMEM((2, 2, tr, SMALL_W), F32),
        ] + ([pltpu.VMEM((CHUNK, GLA_QK), F32)] * 2 if exact_gla else []),
        compiler_params=pltpu.CompilerParams(dimension_semantics=("arbitrary",), vmem_limit_bytes=V7X_VMEM_LIMIT),
        name="chunk_pre",
    )(zb, zb, zb, zs, za, cw, up_bd, ub, alv, dtbv)


def _scan_kernel(*refs, ch):
    ins, (of_ref, ob_ref, sa_ref, sb_ref) = refs[:12], refs[12:]
    per_dir = [ins[0:6], ins[6:12]]
    o_refs = [of_ref, ob_ref]
    j = pl.program_id(1)

    @pl.when(j == 0)
    def _():
        sa_ref[...] = jnp.zeros_like(sa_ref)
        sb_ref[...] = jnp.zeros_like(sb_ref)

    r = lax.broadcasted_iota(jnp.int32, (GLA_DK, GLA_DK), 0)
    cc = lax.broadcasted_iota(jnp.int32, (GLA_DK, GLA_DK), 1)
    eye = cc == r

    hd = [(d, h) for d in range(N_DIR) for h in range(GDN_HEADS)]
    s_b = [sb_ref[d, h] for d, h in hd]
    s_a = [sa_ref[d, h] for d, h in hd]
    for s in range(ch):
        ci = [s, ch - 1 - s]
        rows = [slice(c * CHUNK, (c + 1) * CHUNK) for c in ci]
        res_b = [_dg(per_dir[d][0][ci[d], h], s_b[i].astype(BF16)) for i, (d, h) in enumerate(hd)]
        for i, (d, h) in enumerate(hd):
            e = per_dir[d][2][ci[d], :, h:h + 1]
            s_b[i] = s_b[i] * e - res_b[i][:GDN_DK] + per_dir[d][1][ci[d], h].astype(F32)
            o_refs[d][rows[d], GLA_V + h * GDN_DV:GLA_V + (h + 1) * GDN_DV] = res_b[i][GDN_DK:].astype(BF16)
        res_a = [_dg(per_dir[d][3][ci[d], :, h * GLA_DK:(h + 1) * GLA_DK], s_a[i].astype(BF16))
                 for i, (d, h) in enumerate(hd)]
        for i, (d, h) in enumerate(hd):
            e_row = per_dir[d][5][ci[d], :, h * GLA_DK:(h + 1) * GLA_DK]
            ecol = jnp.sum(jnp.where(eye, e_row, 0.0), axis=1, keepdims=True)
            s_a[i] = s_a[i] * ecol + per_dir[d][4][ci[d], :, h * GLA_DV:(h + 1) * GLA_DV].astype(F32)
            o_refs[d][rows[d], h * GLA_DV:(h + 1) * GLA_DV] = res_a[i].astype(BF16)
    for i, (d, h) in enumerate(hd):
        sb_ref[d, h] = s_b[i]
        sa_ref[d, h] = s_a[i]


def _scan_call(lhs, cg, eg, qe, ca, er, n_lat, ch):
    b, n_all = lhs.shape[0], lhs.shape[1]
    nt_all, nt_lat = n_all // ch, n_lat // ch
    nt_ctx = nt_all - nt_lat
    tr = ch * CHUNK

    def in_blk(dd, j):
        if dd == 0:
            return jnp.where(j < nt_ctx, nt_lat + j, j - nt_ctx)
        return nt_all - 1 - j

    def out_blk(dd, j):
        jj = jnp.maximum(j, nt_ctx)
        return jj - nt_ctx if dd == 0 else nt_all - 1 - jj

    def spec(arr, dd):
        tail = arr.shape[3:]
        return pl.BlockSpec((None, ch, None) + tail,
                            lambda bi, j: (bi, in_blk(dd, j), dd) + (0,) * len(tail))

    arrs = (lhs, cg, eg, qe, ca, er)
    in_specs = [spec(a, dd) for dd in range(N_DIR) for a in arrs]
    out_spec = lambda dd: pl.BlockSpec((None, tr, O_W), lambda bi, j: (bi, out_blk(dd, j), 0))
    o_shape = jax.ShapeDtypeStruct((b, n_lat * CHUNK, O_W), BF16)
    return pl.pallas_call(
        functools.partial(_scan_kernel, ch=ch),
        grid=(b, nt_all),
        in_specs=in_specs,
        out_specs=[out_spec(0), out_spec(1)],
        out_shape=[o_shape, o_shape],
        scratch_shapes=[
            pltpu.VMEM((N_DIR, GLA_HEADS, GLA_DK, GLA_DV), F32),
            pltpu.VMEM((N_DIR, GDN_HEADS, GDN_DK, GDN_DV), F32),
        ],
        compiler_params=pltpu.CompilerParams(
            dimension_semantics=("arbitrary", "arbitrary"), vmem_limit_bytes=V7X_VMEM_LIMIT),
        name="state_scan",
    )(*(arrs + arrs))


def _merge_kernel(oi_ref, of_ref, ob_ref, zg_ref, x_ref, mod_ref, wa_ref, wb_ref, wo_ref, ga_ref, gb_ref,
                  bg_ref, fg_ref, y_ref, *, d_model):
    o = oi_ref[...] + of_ref[...].astype(F32) + ob_ref[...].astype(F32)
    zg = zg_ref[...].astype(F32)

    def head_norm(xh, gain):
        ms = jnp.mean(xh * xh, axis=-1, keepdims=True)
        return xh * lax.rsqrt(ms + EPS) * gain

    ya = jnp.concatenate(
        [head_norm(o[:, h * GLA_DV:(h + 1) * GLA_DV], ga_ref[...]) for h in range(GLA_HEADS)], axis=1)
    yb = jnp.concatenate(
        [head_norm(o[:, GLA_V + h * GDN_DV:GLA_V + (h + 1) * GDN_DV], gb_ref[...]) for h in range(GDN_HEADS)],
        axis=1)
    ya = ya * _silu(zg[:, :GLA_V])
    yb = yb * _silu(zg[:, GLA_V:GLA_V + GDN_V])
    pa = _dg(ya.astype(BF16), wa_ref[...])
    pb = _dg(yb.astype(BF16), wb_ref[...])
    gates = _sigmoid(zg[:, ZG_W:] + bg_ref[...])
    merged = gates[:, :d_model] * pa + gates[:, d_model:] * pb
    out = _dg(merged.astype(BF16), wo_ref[...])
    gate = mod_ref[...][:, 2 * d_model:]
    xn = x_ref[...] + gate * out
    ms = jnp.mean(xn * xn, axis=-1, keepdims=True)
    y_ref[...] = xn * lax.rsqrt(ms + EPS) * fg_ref[...]


def _merge_call(oi, of, ob, zg, x, mod3, wa, wb, wo, ga, gb, bg, fg):
    b, seq, d = x.shape
    tm = next(rows for rows in (512, 256, 128, CHUNK) if seq % rows == 0)
    kern = functools.partial(_merge_kernel, d_model=d)
    full = lambda a: pl.BlockSpec(a.shape, lambda bi, t: (0,) * a.ndim)
    rows = lambda width: pl.BlockSpec((None, tm, width), lambda bi, t: (bi, t, 0))
    return pl.pallas_call(
        kern,
        grid=(b, seq // tm),
        in_specs=[
            rows(O_W), rows(O_W), rows(O_W), rows(zg.shape[2]), rows(d),
            pl.BlockSpec((None, 1, 3 * d), lambda bi, t: (bi, 0, 0)),
            full(wa), full(wb), full(wo), full(ga), full(gb), full(bg), full(fg),
        ],
        out_specs=rows(d),
        out_shape=jax.ShapeDtypeStruct((b, seq, d), x.dtype),
        compiler_params=pltpu.CompilerParams(
            dimension_semantics=("arbitrary", "arbitrary"), vmem_limit_bytes=V7X_VMEM_LIMIT),
        name="merge_out",
    )(oi, of, ob, zg, x, mod3, wa, wb, wo, ga, gb, bg, fg)


def _w_in_groups(d):
    o = 0
    offs = []
    for size in (GLA_QK, GLA_QK, GLA_V, GLA_V, N_DIR * GLA_RANK, GDN_QK, GDN_QK, GDN_V, GDN_V,
                 N_DIR * GDN_HEADS, N_DIR * GDN_HEADS, 2 * d):
        offs.append((o, size))
        o += size
    aq, ak, av, ag, alr, bq, bk, bv, bg, bbeta, bdec, mg = offs
    return [aq, ak, av, bq, bk, bv, ag, bg, mg, alr, bbeta, bdec]


def _regroup_kernel(w_ref, o_ref, *, groups):
    pos = 0
    small = []
    for start, width in groups:
        piece = w_ref[:, start:start + width]
        if width % SMALL_W == 0:
            o_ref[:, pos:pos + width] = piece.astype(BF16)
            pos += width
        else:
            small.append(piece)
    used = sum(p.shape[1] for p in small)
    small.append(jnp.zeros((w_ref.shape[0], SMALL_W - used), F32))
    o_ref[:, pos:pos + SMALL_W] = jnp.concatenate(small, axis=1).astype(BF16)


def _regroup_w_in(w_in):
    _, d, n = w_in.shape
    groups = _w_in_groups(d)
    n_out = sum(wd for _, wd in groups if wd % SMALL_W == 0) + SMALL_W
    rows = 128
    return pl.pallas_call(
        functools.partial(_regroup_kernel, groups=groups),
        grid=(d // rows,),
        in_specs=[pl.BlockSpec((None, rows, n), lambda i: (0, i, 0))],
        out_specs=pl.BlockSpec((rows, n_out), lambda i: (i, 0)),
        out_shape=jax.ShapeDtypeStruct((d, n_out), BF16),
        compiler_params=pltpu.CompilerParams(dimension_semantics=("arbitrary",), vmem_limit_bytes=V7X_VMEM_LIMIT),
        name="regroup_w_in",
    )(w_in)


def kernel(x, c, ctx, c_ctx, w_mod, b_mod, norm_g, w_in, gla_up, gla_ub, gla_onorm, gdn_conv, gdn_a_log,
           gdn_dt_bias, gdn_onorm, w_gla_out, w_gdn_out, b_gate, w_o, final_g):
    b, seq, d = x.shape
    assert w_mod.shape[0] == 1, "single-layer block"
    assert seq % CHUNK == 0 and ctx.shape[1] % CHUNK == 0 and seq % ctx.shape[1] == 0
    n_lat = seq // CHUNK
    ch = _tile_chunks(n_lat, ctx.shape[1] // CHUNK)

    rows = -(-(b + 1) // 8) * 8
    cc = jnp.concatenate([c, c_ctx[None, :], jnp.zeros((rows - b - 1, d), F32)], axis=0)
    mod = _mod_call(cc, w_mod[0], b_mod[0][None, :])
    mod3 = mod[:, None, :]

    w_r = _regroup_w_in(w_in)
    za, zb, zg, zs, nrm = _proj_call(x, ctx, mod3, norm_g[0][None, :], w_r)

    cw = gdn_conv[0].reshape(9, CONV_CH)
    up_bd = jnp.concatenate(
        [jnp.pad(gla_up[0, n], ((0, 0), (n * GLA_QK, (N_DIR - 1 - n) * GLA_QK))) for n in range(N_DIR)], axis=0)
    up_bd = jnp.pad(up_bd, ((0, SMALL_W - N_DIR * GLA_RANK), (0, 0)))
    ub = gla_ub[0].reshape(1, N_DIR * GLA_QK)
    nd = N_DIR * GDN_HEADS
    lane_pad = ((0, 0), (LANE_DEC, SMALL_W - LANE_DEC - nd))
    alv = jnp.pad(gdn_a_log[0].reshape(1, nd), lane_pad)
    dtbv = jnp.pad(gdn_dt_bias[0].reshape(1, nd), lane_pad)
    wa, wb, wo = w_gla_out[0].astype(BF16), w_gdn_out[0].astype(BF16), w_o[0].astype(BF16)

    def mixers_and_merge(exact_gla):
        def run(za, zb, zg, zs):
            lhs, cg, eg, qe, ca, er, oi = _pre_call(zb, za, zs, cw, up_bd, ub, alv, dtbv, n_lat, ch, exact_gla)
            of, ob = _scan_call(lhs, cg, eg, qe, ca, er, n_lat, ch)
            return _merge_call(oi, of, ob, zg, x, mod3, wa, wb, wo, gla_onorm[0][None, :], gdn_onorm[0][None, :],
                               b_gate[0][None, :], final_g[None, :])
        return run

    alr_norm = jnp.sqrt(jnp.max(nrm[:, :, 0, :N_DIR], axis=(0, 1)))
    up_norm = jnp.sqrt(jnp.max(jnp.sum(gla_up[0] * gla_up[0], axis=1), axis=-1))
    decay_bound = CHUNK * (alr_norm * up_norm + jnp.max(jnp.abs(gla_ub[0]), axis=-1) + jnp.log(2.0)) / GLA_TAU
    return lax.cond(jnp.max(decay_bound) > GLA_FACTOR_LIMIT, mixers_and_merge(True), mixers_and_merge(False),
                    za, zb, zg, zs)
```

```python
import functools

import jax
import jax.numpy as jnp
from jax import lax
from jax.experimental import pallas as pl
from jax.experimental.pallas import tpu as pltpu

F32 = jnp.float32
BF16 = jnp.bfloat16

CHUNK = 64
N_DIR = 2
GLA_HEADS, GLA_DK, GLA_DV, GLA_RANK, GLA_TAU = 4, 64, 128, 16, 16.0
GLA_QK, GLA_V = GLA_HEADS * GLA_DK, GLA_HEADS * GLA_DV
GDN_HEADS, GDN_DK, GDN_DV = 4, 128, 128
GDN_QK, GDN_V = GDN_HEADS * GDN_DK, GDN_HEADS * GDN_DV
CONV_CH = 2 * GDN_QK + GDN_V
EPS = 1e-6
GLA_FACTOR_LIMIT = 80.0
O_W = GLA_V + GDN_V

LANE_ALR = 0
LANE_BETA = N_DIR * GLA_RANK
LANE_DEC = LANE_BETA + N_DIR * GDN_HEADS
SMALL_W = 128

ZA_W = 2 * GLA_QK + GLA_V
ZB_W = CONV_CH
ZG_W = GLA_V + GDN_V

V7X_VMEM_LIMIT = 56 * 1024 * 1024

_NN = (((1,), (0,)), ((), ()))
_NT = (((1,), (1,)), ((), ()))
_TN = (((0,), (0,)), ((), ()))


def _dg(a, b, dims=_NN):
    return lax.dot_general(a, b, dims, preferred_element_type=F32)


def _dot3(a, b, dims=_NN):
    ah = a.astype(BF16)
    al = (a - ah.astype(F32)).astype(BF16)
    bh = b.astype(BF16)
    bl = (b - bh.astype(F32)).astype(BF16)
    return _dg(ah, bh, dims) + _dg(ah, bl, dims) + _dg(al, bh, dims)


def _cum(mask_bf, x, passes):
    acc = None
    rem = x
    for _ in range(passes):
        piece = rem.astype(BF16)
        term = _dg(mask_bf, piece)
        acc = term if acc is None else acc + term
        rem = rem - piece.astype(F32)
    return acc


def _sigmoid(x):
    return 1.0 / (1.0 + jnp.exp(-x))


def _silu(x):
    return x * _sigmoid(x)


def _softplus(x):
    return jnp.maximum(x, 0.0) + jnp.log1p(jnp.exp(-jnp.abs(x)))


def _tile_chunks(n_lat, n_ctx):
    for ch in (4, 2, 1):
        if n_lat % ch == 0 and n_ctx % ch == 0:
            return ch


def _mod_kernel(c_ref, w_ref, b_ref, o_ref):
    o_ref[...] = _dot3(_silu(c_ref[...]), w_ref[...]) + b_ref[...]


def _mod_call(cc, w_mod, b_mod):
    rows, d = cc.shape
    n = w_mod.shape[1]
    bn = d
    return pl.pallas_call(
        _mod_kernel,
        grid=(n // bn,),
        in_specs=[
            pl.BlockSpec((rows, d), lambda i: (0, 0)),
            pl.BlockSpec((d, bn), lambda i: (0, i)),
            pl.BlockSpec((1, bn), lambda i: (0, i)),
        ],
        out_specs=pl.BlockSpec((rows, bn), lambda i: (0, i)),
        out_shape=jax.ShapeDtypeStruct((rows, n), F32),
        compiler_params=pltpu.CompilerParams(dimension_semantics=("arbitrary",), vmem_limit_bytes=V7X_VMEM_LIMIT),
        name="adaln_mod",
    )(cc, w_mod, b_mod)


def _proj_kernel(x_ref, ctx_ref, mod_ref, g_ref, w_ref, za_ref, zb_ref, zg_ref, zs_ref, nrm_ref, *, d_model):
    t = pl.program_id(1)
    xin = jnp.where(t == 0, ctx_ref[...], x_ref[...])
    ms = jnp.mean(xin * xin, axis=-1, keepdims=True)
    y = xin * lax.rsqrt(ms + EPS) * g_ref[...]
    mod = mod_ref[...]
    shift = mod[:, :d_model]
    scale = mod[:, d_model:2 * d_model]
    h = (y * (1.0 + scale) + shift).astype(BF16)
    o0, o1, o2, o3 = 0, ZA_W, ZA_W + ZB_W, ZA_W + ZB_W + ZG_W + 2 * d_model
    zs = _dg(h, w_ref[:, o3:o3 + SMALL_W])
    zs_ref[...] = zs
    lane = lax.broadcasted_iota(jnp.int32, (1, SMALL_W), 1)
    sq = zs * zs
    nrm = jnp.zeros((1, SMALL_W), F32)
    for n in range(N_DIR):
        in_dir = (lane >= LANE_ALR + n * GLA_RANK) & (lane < LANE_ALR + (n + 1) * GLA_RANK)
        worst = jnp.max(jnp.sum(jnp.where(in_dir, sq, 0.0), axis=1, keepdims=True), axis=0, keepdims=True)
        nrm = jnp.where(lane == n, worst, nrm)
    nrm_ref[...] = jnp.broadcast_to(nrm, nrm_ref.shape)
    za_ref[...] = _dg(h, w_ref[:, o0:o1])
    zb_ref[...] = _dg(h, w_ref[:, o1:o2])

    @pl.when(t > 0)
    def _():
        zg_ref[...] = _dg(h, w_ref[:, o2:o3]).astype(zg_ref.dtype)


def _proj_call(x, ctx, mod3, norm_g, w_r):
    b, seq, d = x.shape
    ctx_len = ctx.shape[1]
    tm = ctx_len
    n_lat = seq // tm
    tot = seq + ctx_len
    zgw = ZG_W + 2 * d
    n_all = w_r.shape[1]

    def row_blk(bi, t):
        return jnp.where(t == 0, n_lat, t - 1)

    kern = functools.partial(_proj_kernel, d_model=d)
    return pl.pallas_call(
        kern,
        grid=(b, n_lat + 1),
        in_specs=[
            pl.BlockSpec((None, tm, d), lambda bi, t: (bi, jnp.maximum(t - 1, 0), 0)),
            pl.BlockSpec((None, tm, d), lambda bi, t: (bi, 0, 0)),
            pl.BlockSpec((None, 1, 3 * d), lambda bi, t: (jnp.where(t == 0, b, bi), 0, 0)),
            pl.BlockSpec((1, d), lambda bi, t: (0, 0)),
            pl.BlockSpec((d, n_all), lambda bi, t: (0, 0), pipeline_mode=pl.Buffered(1)),
        ],
        out_specs=[
            pl.BlockSpec((None, tm, ZA_W), lambda bi, t: (bi, row_blk(bi, t), 0)),
            pl.BlockSpec((None, tm, ZB_W), lambda bi, t: (bi, row_blk(bi, t), 0)),
            pl.BlockSpec((None, tm, zgw), lambda bi, t: (bi, jnp.maximum(t - 1, 0), 0)),
            pl.BlockSpec((None, tm, SMALL_W), lambda bi, t: (bi, row_blk(bi, t), 0)),
            pl.BlockSpec((None, None, 8, SMALL_W), lambda bi, t: (bi, t, 0, 0)),
        ],
        out_shape=[
            jax.ShapeDtypeStruct((b, tot, ZA_W), F32),
            jax.ShapeDtypeStruct((b, tot, ZB_W), F32),
            jax.ShapeDtypeStruct((b, seq, zgw), BF16),
            jax.ShapeDtypeStruct((b, tot, SMALL_W), F32),
            jax.ShapeDtypeStruct((b, n_lat + 1, 8, SMALL_W), F32),
        ],
        compiler_params=pltpu.CompilerParams(
            dimension_semantics=("arbitrary", "arbitrary"), vmem_limit_bytes=V7X_VMEM_LIMIT),
        name="in_proj",
    )(x, ctx, mod3, norm_g, w_r)


INV_BLOCK = 16
FRONT_PERIOD = 12


def _stage(fn, items):
    out = []
    for it in items:
        out.append(fn(*it) if isinstance(it, tuple) else fn(it))
        yield
    return out


def _lane_block_diag(pair_bf):
    left = lax.broadcasted_iota(jnp.int32, (1, 2 * CHUNK), 1) < CHUNK
    zero = jnp.zeros_like(pair_bf)
    return jnp.concatenate([jnp.where(left, pair_bf, zero), jnp.where(left, zero, pair_bf)], axis=0)


def _diag_block_inverses(neg_a_list, eye2, blk2):
    nds = [na * blk2 for na in neg_a_list]
    naos = [(na - nd).astype(BF16) for na, nd in zip(neg_a_list, nds)]
    ds = [eye2 + nd for nd in nds]
    nbs = [nd.astype(BF16) for nd in nds]
    rhs = [_lane_block_diag(nb) for nb in nbs]
    for _ in range(INV_BLOCK.bit_length() - 2):
        ns = yield from _stage(_dg, list(zip(nbs, rhs)))
        nbs = [n.astype(BF16) for n in ns]
        rhs = [_lane_block_diag(nb) for nb in nbs]
        ds = yield from _stage(lambda d, r: d + _dg(d.astype(BF16), r), list(zip(ds, rhs)))
    return [d.astype(BF16) for d in ds], naos


def _finish_inverses(dbs, naos, eye2):
    ms = yield from _stage(lambda db, nao: _dg(db, _lane_block_diag(nao)), list(zip(dbs, naos)))
    mbs = [m.astype(BF16) for m in ms]
    m2s = yield from _stage(lambda mb: _dg(mb, _lane_block_diag(mb)), mbs)
    m3s = yield from _stage(lambda mb, m2: _dg(mb, _lane_block_diag(m2.astype(BF16))), list(zip(mbs, m2s)))
    ss = [(eye2 + m + m2 + m3).astype(BF16) for m, m2, m3 in zip(ms, m2s, m3s)]
    return (yield from _stage(lambda s_, db: _dg(s_, _lane_block_diag(db)), list(zip(ss, dbs))))


def _gla_scores_exact(q, k, bcum, k_ref, b_ref):
    k_ref[...] = k
    b_ref[...] = bcum
    lane_s = lax.broadcasted_iota(jnp.int32, (1, CHUNK), 1)
    rr = lax.broadcasted_iota(jnp.int32, (GLA_QK, SMALL_W), 0)
    ll = lax.broadcasted_iota(jnp.int32, (GLA_QK, SMALL_W), 1)
    head_sum = (jnp.right_shift(rr, GLA_DK.bit_length() - 1) == ll).astype(F32)

    def column(s, acc):
        p = q * jnp.exp(jnp.minimum(bcum - b_ref[pl.ds(s, 1), :], 0.0)) * k_ref[pl.ds(s, 1), :]
        per_head = _dot3(p, head_sum)
        hit = (lane_s == s).astype(F32)
        return tuple(a + per_head[:, h:h + 1] * hit for h, a in enumerate(acc))

    zero = jnp.zeros((CHUNK, CHUNK), F32)
    return lax.fori_loop(0, CHUNK, column, (zero,) * GLA_HEADS)


def _pre_kernel(zp_ref, zc_ref, zn_ref, zs_ref, za_ref, cw_ref, up_ref, ub_ref, al_ref, dtb_ref,
                lhs_ref, cg_ref, eg_ref, qe_ref, ca_ref, er_ref, oi_ref, act_ref, gdec_ref, small_ref, *exact_scratch,
                nt_lat, nt_all, n_tiles, ch, exact_gla):
    g = pl.program_id(0)
    tf = lax.rem(jnp.minimum(g, n_tiles - 1), nt_all)
    slot_f = lax.rem(g, 2)
    slot_b = 1 - slot_f
    rows_of = lambda ci: slice(ci * CHUNK, (ci + 1) * CHUNK)
    dirs = range(N_DIR)
    chunks = range(ch)

    @pl.when(g == 0)
    def _():
        act_ref[1] = jnp.zeros(act_ref.shape[1:], F32)
        gdec_ref[1] = jnp.zeros(gdec_ref.shape[1:], F32)
        small_ref[1] = jnp.zeros(small_ref.shape[1:], F32)

    f_ctx = tf >= nt_lat
    row = lax.broadcasted_iota(jnp.int32, (CHUNK, 1), 0)
    w = cw_ref[...]
    lat_f = jnp.logical_not(f_ctx).astype(F32)
    ctx_f = f_ctx.astype(F32)
    pv = (tf > jnp.where(f_ctx, nt_lat, 0)).astype(F32)
    nv = (tf < jnp.where(f_ctx, nt_all - 1, nt_lat - 1)).astype(F32)
    w_up, w_mid, w_dn = w[0:3] * lat_f, w[3:6], w[6:9] * lat_f
    w_edge_l, w_edge_r = w[3:4] * ctx_f, w[5:6] * ctx_f

    def front_cols(ci, j):
        cols = slice(j * GDN_DK, (j + 1) * GDN_DK)
        prev = zp_ref[:, cols] * pv if ci == 0 else zc_ref[rows_of(ci - 1), cols]
        nxt = zn_ref[:, cols] * nv if ci == ch - 1 else zc_ref[rows_of(ci + 1), cols]
        cur = zc_ref[rows_of(ci), cols]

        def col(dc):
            return (w_up[dc:dc + 1, cols] * prev + w_mid[dc:dc + 1, cols] * cur + w_dn[dc:dc + 1, cols] * nxt)

        left = jnp.where(row >= 1, pltpu.roll(col(0), 1, 0), w_edge_l[:, cols] * prev[CHUNK - 1:CHUNK])
        right = jnp.where(row <= CHUNK - 2, pltpu.roll(col(2), CHUNK - 1, 0), w_edge_r[:, cols] * nxt[0:1])
        a = _silu(col(1) + left + right)
        if j < 2 * GDN_HEADS:
            inv = lax.rsqrt(jnp.sum(a * a, axis=-1, keepdims=True) + EPS)
            a = a * (inv * (GDN_DK ** -0.5 if j < GDN_HEADS else 1.0))
        act_ref[slot_f, rows_of(ci), cols] = a

    def front_small(ci):
        zs = zs_ref[rows_of(ci), :]
        logits = _dot3(zs, up_ref[...]) + ub_ref[...]
        gdec_ref[slot_f, rows_of(ci), :] = (
            (jnp.minimum(logits, 0.0) - jnp.log1p(jnp.exp(-jnp.abs(logits)))) * (1.0 / GLA_TAU))
        small_ref[slot_f, 0, rows_of(ci), :] = _sigmoid(zs)
        small_ref[slot_f, 1, rows_of(ci), :] = -jnp.exp(al_ref[...]) * _softplus(zs + dtb_ref[...])

    pending_front = []
    for ci in chunks:
        pending_front.append(functools.partial(front_small, ci))
        pending_front.extend(functools.partial(front_cols, ci, j) for j in range(ZB_W // GDN_DK))
    ticks = [0]

    def tick():
        ticks[0] += 1
        if pending_front and ticks[0] % FRONT_PERIOD == 0:
            pending_front.pop(0)()

    def drain_front():
        while pending_front:
            pending_front.pop(0)()

    r = lax.broadcasted_iota(jnp.int32, (CHUNK, CHUNK), 0)
    cc = lax.broadcasted_iota(jnp.int32, (CHUNK, CHUNK), 1)
    incl = [cc <= r, cc >= r]
    r2 = lax.broadcasted_iota(jnp.int32, (CHUNK, 2 * CHUNK), 0)
    c2 = lax.broadcasted_iota(jnp.int32, (CHUNK, 2 * CHUNK), 1)
    s2 = jnp.where(c2 < CHUNK, c2, c2 - CHUNK)
    fwd_half = c2 < CHUNK
    lo2 = jnp.where(fwd_half, s2, r2)
    hi2 = jnp.where(fwd_half, r2, s2)
    incl2 = lo2 <= hi2
    strict2_f = (lo2 < hi2).astype(F32)
    eye2 = (s2 == r2).astype(F32)
    blk_shift = INV_BLOCK.bit_length() - 1
    blk2 = (jnp.right_shift(r2, blk_shift) == jnp.right_shift(s2, blk_shift)).astype(F32)
    cum_bf = [m.astype(BF16) for m in incl]
    last = [lambda x: x[CHUNK - 1:CHUNK], lambda x: x[0:1]]
    lane = lax.broadcasted_iota(jnp.int32, (1, SMALL_W), 1)
    sk = lambda h: slice(h * GLA_DK, (h + 1) * GLA_DK)

    def back():
        cs = list(chunks)
        cd = [(ci, d) for ci in cs for d in dirs]
        za = {ci: za_ref[rows_of(ci), :] for ci in cs}
        v_bf = {ci: [za[ci][:, 2 * GLA_QK + h * GLA_DV:2 * GLA_QK + (h + 1) * GLA_DV].astype(BF16)
                     for h in range(GLA_HEADS)] for ci in cs}

        def gla_front(ci, d):
            q = za[ci][:, :GLA_QK] * (GLA_DK ** -0.5)
            k = za[ci][:, GLA_QK:2 * GLA_QK]
            bcum = _cum(cum_bf[d], gdec_ref[slot_b, rows_of(ci), d * GLA_QK:(d + 1) * GLA_QK], 2)
            bl = last[d](bcum)
            qe = (q * jnp.exp(bcum)).astype(BF16)
            qe_ref[ci, d] = qe
            er_ref[ci, d] = jnp.exp(bl)
            if exact_gla:
                scores = _gla_scores_exact(q, k, bcum, *exact_scratch)
            else:
                scores = (k * jnp.exp(-bcum)).astype(BF16)
            return qe, scores, (k * jnp.exp(bl - bcum)).astype(BF16)

        qkk = dict(zip(cd, (yield from _stage(gla_front, cd))))
        hd_a = [(ci, d, h) for ci, d in cd for h in range(GLA_HEADS)]

        def masked_scores(ci, d, h):
            qe, scores, _ = qkk[ci, d]
            full = scores[h] if exact_gla else _dg(qe[:, sk(h)], scores[:, sk(h)], _NT)
            return jnp.where(incl[d], full, 0.0).astype(BF16)

        att = yield from _stage(masked_scores, hd_a)
        oi_a = dict(zip(hd_a, (yield from _stage(lambda i: _dg(att[i], v_bf[hd_a[i][0]][hd_a[i][2]]),
                                                  list(range(len(hd_a)))))))

        def gla_c(ci, d, h):
            ca_ref[ci, d, :, h * GLA_DV:(h + 1) * GLA_DV] = _dg(qkk[ci, d][2][:, sk(h)], v_bf[ci][h], _TN).astype(BF16)

        yield from _stage(gla_c, hd_a)

        ch_h = [(ci, h) for ci in cs for h in range(GDN_HEADS)]
        qs = {(ci, h): act_ref[slot_b, rows_of(ci), h * GDN_DK:(h + 1) * GDN_DK] for ci, h in ch_h}
        ks = {(ci, h): act_ref[slot_b, rows_of(ci), GDN_QK + h * GDN_DK:GDN_QK + (h + 1) * GDN_DK] for ci, h in ch_h}
        vs = {(ci, h): act_ref[slot_b, rows_of(ci), 2 * GDN_QK + h * GDN_DV:2 * GDN_QK + (h + 1) * GDN_DV]
              for ci, h in ch_h}
        beta_all = {ci: small_ref[slot_b, 0, rows_of(ci), :] for ci in cs}
        gam_all = dict(zip(cd, (yield from _stage(
            lambda ci, d: _cum(cum_bf[d], small_ref[slot_b, 1, rows_of(ci), :], 3), cd))))
        gam_t = {key: gam_all[key].T for key in cd}
        raw = dict(zip(ch_h, (yield from _stage(
            lambda ci, h: _dg(jnp.concatenate([ks[ci, h], qs[ci, h]], axis=0).astype(BF16),
                              jnp.concatenate([ks[ci, h], ks[ci, h]], axis=0).astype(BF16), _NT), ch_h))))
        hd_b = [(ci, d, h) for ci, d in cd for h in range(GDN_HEADS)]
        n_b = len(hd_b)
        left = lax.broadcasted_iota(jnp.int32, (1, 2 * CHUNK), 1) < CHUNK

        def pair_terms(ci, h):
            gcs, betas, gls, grs = [], [], [], []
            for d in dirs:
                ld, lb = LANE_DEC + d * GDN_HEADS + h, LANE_BETA + d * GDN_HEADS + h
                gcs.append(gam_all[ci, d][:, ld:ld + 1])
                grs.append(gam_t[ci, d][ld:ld + 1, :])
                betas.append(beta_all[ci][:, lb:lb + 1])
                gls.append(last[d](gcs[-1]))
            gc2 = jnp.where(left, gcs[0], gcs[1])
            dm2 = jnp.where(incl2, jnp.exp(jnp.minimum(gc2 - jnp.concatenate(grs, axis=1), 0.0)), 0.0)
            neg_a2 = raw[ci, h][:CHUNK] * (-jnp.where(left, betas[0], betas[1])) * (dm2 * strict2_f)
            aqk2 = (raw[ci, h][CHUNK:] * dm2).astype(BF16)
            return gcs, betas, gls, neg_a2, aqk2

        terms = dict(zip(ch_h, (yield from _stage(pair_terms, ch_h))))
        gam_c = [terms[ci, h][0][d] for ci, d, h in hd_b]
        beta_c = [terms[ci, h][1][d] for ci, d, h in hd_b]
        gl = [terms[ci, h][2][d] for ci, d, h in hd_b]
        dbs, naos = yield from _diag_block_inverses([terms[key][3] for key in ch_h], eye2, blk2)
        tmat2 = dict(zip(ch_h, (yield from _finish_inverses(dbs, naos, eye2))))
        egc = [jnp.exp(gc) for gc in gam_c]

        def on_dir(x, d):
            zero = jnp.zeros_like(x)
            return jnp.concatenate([x, zero] if d == 0 else [zero, x], axis=0)

        uw = yield from _stage(
            lambda i: _dg(tmat2[hd_b[i][0], hd_b[i][2]].astype(BF16), on_dir(jnp.concatenate(
                [vs[hd_b[i][0], hd_b[i][2]] * beta_c[i], ks[hd_b[i][0], hd_b[i][2]] * (beta_c[i] * egc[i])],
                axis=1).astype(BF16), hd_b[i][1])).astype(BF16), list(range(n_b)))
        aw = yield from _stage(
            lambda i: _dg(terms[hd_b[i][0], hd_b[i][2]][4], on_dir(uw[i], hd_b[i][1])), list(range(n_b)))
        pc = yield from _stage(
            lambda i: _dg((ks[hd_b[i][0], hd_b[i][2]] * jnp.exp(gl[i] - gam_c[i])).astype(BF16), uw[i], _TN),
            list(range(n_b)))
        for ci, d in cd:
            row_e = jnp.zeros((1, SMALL_W), F32)
            for h in range(GDN_HEADS):
                row_e = jnp.where(lane == h, jnp.exp(gl[hd_b.index((ci, d, h))]), row_e)
            eg_ref[ci, d] = row_e

        def store_chain(i):
            ci, d, h = hd_b[i]
            lhs_ref[ci, d, h, :GDN_DK] = pc[i][:, GDN_DV:].astype(BF16)
            lhs_ref[ci, d, h, GDN_DK:] = (qs[ci, h] * egc[i] - aw[i][:, GDN_DV:]).astype(BF16)
            cg_ref[ci, d, h] = pc[i][:, :GDN_DV].astype(BF16)

        yield from _stage(store_chain, list(range(n_b)))
        for ci in cs:
            for h in range(GLA_HEADS):
                oi_ref[rows_of(ci), h * GLA_DV:(h + 1) * GLA_DV] = oi_a[ci, 0, h] + oi_a[ci, 1, h]
            for h in range(GDN_HEADS):
                i0, i1 = hd_b.index((ci, 0, h)), hd_b.index((ci, 1, h))
                oi_ref[rows_of(ci), GLA_V + h * GDN_DV:GLA_V + (h + 1) * GDN_DV] = (
                    aw[i0][:, :GDN_DV] + aw[i1][:, :GDN_DV])
            yield

    for _ in back():
        tick()
    drain_front()


def _pre_call(zb, za, zs, cw, up_bd, ub, alv, dtbv, n_lat, ch, exact_gla):
    b, tot, _ = zb.shape
    n_all = tot // CHUNK
    nt_lat, nt_all = n_lat // ch, n_all // ch
    n_tiles = b * nt_all
    tr = ch * CHUNK

    def front_tile(g):
        gf = jnp.minimum(g, n_tiles - 1)
        return lax.div(gf, nt_all), lax.rem(gf, nt_all)

    def back_tile(g):
        gb = jnp.maximum(g - 1, 0)
        return lax.div(gb, nt_all), lax.rem(gb, nt_all)

    def prev_blk(g):
        bi, t = front_tile(g)
        return bi, jnp.where(t >= nt_lat, jnp.maximum(t * ch - 1, n_lat), jnp.maximum(t * ch - 1, 0)), 0

    def next_blk(g):
        bi, t = front_tile(g)
        return bi, jnp.where(t >= nt_lat, jnp.minimum((t + 1) * ch, n_all - 1),
                             jnp.minimum((t + 1) * ch, n_lat - 1)), 0

    kern = functools.partial(_pre_kernel, nt_lat=nt_lat, nt_all=nt_all, n_tiles=n_tiles, ch=ch,
                             exact_gla=exact_gla)
    full = lambda shape: pl.BlockSpec(shape, lambda g: (0,) * len(shape))
    per_tile = lambda *tail: pl.BlockSpec((None, ch) + tail, lambda g: back_tile(g) + (0,) * len(tail))
    return pl.pallas_call(
        kern,
        grid=(n_tiles + 1,),
        in_specs=[
            pl.BlockSpec((None, CHUNK, ZB_W), prev_blk),
            pl.BlockSpec((None, tr, ZB_W), lambda g: front_tile(g) + (0,)),
            pl.BlockSpec((None, CHUNK, ZB_W), next_blk),
            pl.BlockSpec((None, tr, SMALL_W), lambda g: front_tile(g) + (0,)),
            pl.BlockSpec((None, tr, ZA_W), lambda g: back_tile(g) + (0,)),
            full(cw.shape), full(up_bd.shape), full(ub.shape), full(alv.shape), full(dtbv.shape),
        ],
        out_specs=[
            per_tile(N_DIR, GDN_HEADS, GDN_DK + CHUNK, GDN_DV),
            per_tile(N_DIR, GDN_HEADS, GDN_DK, GDN_DV),
            per_tile(N_DIR, 1, SMALL_W),
            per_tile(N_DIR, CHUNK, GLA_QK),
            per_tile(N_DIR, CHUNK, GLA_V),
            per_tile(N_DIR, 1, GLA_QK),
            pl.BlockSpec((None, tr, O_W), lambda g: back_tile(g) + (0,)),
        ],
        out_shape=[
            jax.ShapeDtypeStruct((b, n_all, N_DIR, GDN_HEADS, GDN_DK + CHUNK, GDN_DV), BF16),
            jax.ShapeDtypeStruct((b, n_all, N_DIR, GDN_HEADS, GDN_DK, GDN_DV), BF16),
            jax.ShapeDtypeStruct((b, n_all, N_DIR, 1, SMALL_W), F32),
            jax.ShapeDtypeStruct((b, n_all, N_DIR, CHUNK, GLA_QK), BF16),
            jax.ShapeDtypeStruct((b, n_all, N_DIR, CHUNK, GLA_V), BF16),
            jax.ShapeDtypeStruct((b, n_all, N_DIR, 1, GLA_QK), F32),
            jax.ShapeDtypeStruct((b, tot, O_W), F32),
        ],
        scratch_shapes=[
            pltpu.VMEM((2, tr, ZB_W), F32),
            pltpu.VMEM((2, tr, N_DIR * GLA_QK), F32),
            pltpu.VMEM((2, 2, tr, SMALL_W), F32),
        ] + ([pltpu.VMEM((CHUNK, GLA_QK), F32)] * 2 if exact_gla else []),
        compiler_params=pltpu.CompilerParams(dimension_semantics=("arbitrary",), vmem_limit_bytes=V7X_VMEM_LIMIT),
        name="chunk_pre",
    )(zb, zb, zb, zs, za, cw, up_bd, ub, alv, dtbv)


def _scan_kernel(*refs, ch):
    ins, (of_ref, ob_ref, sa_ref, sb_ref) = refs[:12], refs[12:]
    per_dir = [ins[0:6], ins[6:12]]
    o_refs = [of_ref, ob_ref]
    j = pl.program_id(1)

    @pl.when(j == 0)
    def _():
        sa_ref[...] = jnp.zeros_like(sa_ref)
        sb_ref[...] = jnp.zeros_like(sb_ref)

    r = lax.broadcasted_iota(jnp.int32, (GLA_DK, GLA_DK), 0)
    cc = lax.broadcasted_iota(jnp.int32, (GLA_DK, GLA_DK), 1)
    eye = cc == r

    hd = [(d, h) for d in range(N_DIR) for h in range(GDN_HEADS)]
    s_b = [sb_ref[d, h] for d, h in hd]
    s_a = [sa_ref[d, h] for d, h in hd]
    for s in range(ch):
        ci = [s, ch - 1 - s]
        rows = [slice(c * CHUNK, (c + 1) * CHUNK) for c in ci]
        res_b = [_dg(per_dir[d][0][ci[d], h], s_b[i].astype(BF16)) for i, (d, h) in enumerate(hd)]
        for i, (d, h) in enumerate(hd):
            e = per_dir[d][2][ci[d], :, h:h + 1]
            s_b[i] = s_b[i] * e - res_b[i][:GDN_DK] + per_dir[d][1][ci[d], h].astype(F32)
            o_refs[d][rows[d], GLA_V + h * GDN_DV:GLA_V + (h + 1) * GDN_DV] = res_b[i][GDN_DK:].astype(BF16)
        res_a = [_dg(per_dir[d][3][ci[d], :, h * GLA_DK:(h + 1) * GLA_DK], s_a[i].astype(BF16))
                 for i, (d, h) in enumerate(hd)]
        for i, (d, h) in enumerate(hd):
            e_row = per_dir[d][5][ci[d], :, h * GLA_DK:(h + 1) * GLA_DK]
            ecol = jnp.sum(jnp.where(eye, e_row, 0.0), axis=1, keepdims=True)
            s_a[i] = s_a[i] * ecol + per_dir[d][4][ci[d], :, h * GLA_DV:(h + 1) * GLA_DV].astype(F32)
            o_refs[d][rows[d], h * GLA_DV:(h + 1) * GLA_DV] = res_a[i].astype(BF16)
    for i, (d, h) in enumerate(hd):
        sb_ref[d, h] = s_b[i]
        sa_ref[d, h] = s_a[i]


def _scan_call(lhs, cg, eg, qe, ca, er, n_lat, ch):
    b, n_all = lhs.shape[0], lhs.shape[1]
    nt_all, nt_lat = n_all // ch, n_lat // ch
    nt_ctx = nt_all - nt_lat
    tr = ch * CHUNK

    def in_blk(dd, j):
        if dd == 0:
            return jnp.where(j < nt_ctx, nt_lat + j, j - nt_ctx)
        return nt_all - 1 - j

    def out_blk(dd, j):
        jj = jnp.maximum(j, nt_ctx)
        return jj - nt_ctx if dd == 0 else nt_all - 1 - jj

    def spec(arr, dd):
        tail = arr.shape[3:]
        return pl.BlockSpec((None, ch, None) + tail,
                            lambda bi, j: (bi, in_blk(dd, j), dd) + (0,) * len(tail))

    arrs = (lhs, cg, eg, qe, ca, er)
    in_specs = [spec(a, dd) for dd in range(N_DIR) for a in arrs]
    out_spec = lambda dd: pl.BlockSpec((None, tr, O_W), lambda bi, j: (bi, out_blk(dd, j), 0))
    o_shape = jax.ShapeDtypeStruct((b, n_lat * CHUNK, O_W), BF16)
    return pl.pallas_call(
        functools.partial(_scan_kernel, ch=ch),
        grid=(b, nt_all),
        in_specs=in_specs,
        out_specs=[out_spec(0), out_spec(1)],
        out_shape=[o_shape, o_shape],
        scratch_shapes=[
            pltpu.VMEM((N_DIR, GLA_HEADS, GLA_DK, GLA_DV), F32),
            pltpu.VMEM((N_DIR, GDN_HEADS, GDN_DK, GDN_DV), F32),
        ],
        compiler_params=pltpu.CompilerParams(
            dimension_semantics=("arbitrary", "arbitrary"), vmem_limit_bytes=V7X_VMEM_LIMIT),
        name="state_scan",
    )(*(arrs + arrs))


def _merge_kernel(oi_ref, of_ref, ob_ref, zg_ref, x_ref, mod_ref, wa_ref, wb_ref, wo_ref, ga_ref, gb_ref,
                  bg_ref, fg_ref, y_ref, *, d_model):
    o = oi_ref[...] + of_ref[...].astype(F32) + ob_ref[...].astype(F32)
    zg = zg_ref[...].astype(F32)

    def head_norm(xh, gain):
        ms = jnp.mean(xh * xh, axis=-1, keepdims=True)
        return xh * lax.rsqrt(ms + EPS) * gain

    ya = jnp.concatenate(
        [head_norm(o[:, h * GLA_DV:(h + 1) * GLA_DV], ga_ref[...]) for h in range(GLA_HEADS)], axis=1)
    yb = jnp.concatenate(
        [head_norm(o[:, GLA_V + h * GDN_DV:GLA_V + (h + 1) * GDN_DV], gb_ref[...]) for h in range(GDN_HEADS)],
        axis=1)
    ya = ya * _silu(zg[:, :GLA_V])
    yb = yb * _silu(zg[:, GLA_V:GLA_V + GDN_V])
    pa = _dg(ya.astype(BF16), wa_ref[...])
    pb = _dg(yb.astype(BF16), wb_ref[...])
    gates = _sigmoid(zg[:, ZG_W:] + bg_ref[...])
    merged = gates[:, :d_model] * pa + gates[:, d_model:] * pb
    out = _dg(merged.astype(BF16), wo_ref[...])
    gate = mod_ref[...][:, 2 * d_model:]
    xn = x_ref[...] + gate * out
    ms = jnp.mean(xn * xn, axis=-1, keepdims=True)
    y_ref[...] = xn * lax.rsqrt(ms + EPS) * fg_ref[...]


def _merge_call(oi, of, ob, zg, x, mod3, wa, wb, wo, ga, gb, bg, fg):
    b, seq, d = x.shape
    tm = next(rows for rows in (512, 256, 128, CHUNK) if seq % rows == 0)
    kern = functools.partial(_merge_kernel, d_model=d)
    full = lambda a: pl.BlockSpec(a.shape, lambda bi, t: (0,) * a.ndim)
    rows = lambda width: pl.BlockSpec((None, tm, width), lambda bi, t: (bi, t, 0))
    return pl.pallas_call(
        kern,
        grid=(b, seq // tm),
        in_specs=[
            rows(O_W), rows(O_W), rows(O_W), rows(zg.shape[2]), rows(d),
            pl.BlockSpec((None, 1, 3 * d), lambda bi, t: (bi, 0, 0)),
            full(wa), full(wb), full(wo), full(ga), full(gb), full(bg), full(fg),
        ],
        out_specs=rows(d),
        out_shape=jax.ShapeDtypeStruct((b, seq, d), x.dtype),
        compiler_params=pltpu.CompilerParams(
            dimension_semantics=("arbitrary", "arbitrary"), vmem_limit_bytes=V7X_VMEM_LIMIT),
        name="merge_out",
    )(oi, of, ob, zg, x, mod3, wa, wb, wo, ga, gb, bg, fg)


def _w_in_groups(d):
    o = 0
    offs = []
    for size in (GLA_QK, GLA_QK, GLA_V, GLA_V, N_DIR * GLA_RANK, GDN_QK, GDN_QK, GDN_V, GDN_V,
                 N_DIR * GDN_HEADS, N_DIR * GDN_HEADS, 2 * d):
        offs.append((o, size))
        o += size
    aq, ak, av, ag, alr, bq, bk, bv, bg, bbeta, bdec, mg = offs
    return [aq, ak, av, bq, bk, bv, ag, bg, mg, alr, bbeta, bdec]


def _regroup_kernel(wt_ref, o_ref, *, groups):
    pos = 0
    small = []
    for start, width in groups:
        piece = wt_ref[start:start + width, :]
        if width % SMALL_W == 0:
            o_ref[:, pos:pos + width] = piece.T.astype(BF16)
            pos += width
        else:
            small.append(piece)
    used = sum(p.shape[0] for p in small)
    small.append(jnp.zeros((SMALL_W - used, wt_ref.shape[1]), F32))
    o_ref[:, pos:pos + SMALL_W] = jnp.concatenate(small, axis=0).T.astype(BF16)


def _regroup_w_in(w_in):
    _, d, n = w_in.shape
    groups = _w_in_groups(d)
    n_out = sum(wd for _, wd in groups if wd % SMALL_W == 0) + SMALL_W
    rows = 128
    return pl.pallas_call(
        functools.partial(_regroup_kernel, groups=groups),
        grid=(d // rows,),
        in_specs=[pl.BlockSpec((n, rows), lambda i: (0, i))],
        out_specs=pl.BlockSpec((rows, n_out), lambda i: (i, 0)),
        out_shape=jax.ShapeDtypeStruct((d, n_out), BF16),
        compiler_params=pltpu.CompilerParams(dimension_semantics=("arbitrary",), vmem_limit_bytes=V7X_VMEM_LIMIT),
        name="regroup_w_in",
    )(jnp.transpose(w_in[0]))


def kernel(x, c, ctx, c_ctx, w_mod, b_mod, norm_g, w_in, gla_up, gla_ub, gla_onorm, gdn_conv, gdn_a_log,
           gdn_dt_bias, gdn_onorm, w_gla_out, w_gdn_out, b_gate, w_o, final_g):
    b, seq, d = x.shape
    assert w_mod.shape[0] == 1, "single-layer block"
    assert seq % CHUNK == 0 and ctx.shape[1] % CHUNK == 0 and seq % ctx.shape[1] == 0
    n_lat = seq // CHUNK
    ch = _tile_chunks(n_lat, ctx.shape[1] // CHUNK)

    rows = -(-(b + 1) // 8) * 8
    cc = jnp.concatenate([c, c_ctx[None, :], jnp.zeros((rows - b - 1, d), F32)], axis=0)
    mod = _mod_call(cc, w_mod[0], b_mod[0][None, :])
    mod3 = mod[:, None, :]

    w_r = _regroup_w_in(w_in)
    za, zb, zg, zs, nrm = _proj_call(x, ctx, mod3, norm_g[0][None, :], w_r)

    cw = gdn_conv[0].reshape(9, CONV_CH)
    up_bd = jnp.concatenate(
        [jnp.pad(gla_up[0, n], ((0, 0), (n * GLA_QK, (N_DIR - 1 - n) * GLA_QK))) for n in range(N_DIR)], axis=0)
    up_bd = jnp.pad(up_bd, ((0, SMALL_W - N_DIR * GLA_RANK), (0, 0)))
    ub = gla_ub[0].reshape(1, N_DIR * GLA_QK)
    nd = N_DIR * GDN_HEADS
    lane_pad = ((0, 0), (LANE_DEC, SMALL_W - LANE_DEC - nd))
    alv = jnp.pad(gdn_a_log[0].reshape(1, nd), lane_pad)
    dtbv = jnp.pad(gdn_dt_bias[0].reshape(1, nd), lane_pad)
    wa, wb, wo = w_gla_out[0].astype(BF16), w_gdn_out[0].astype(BF16), w_o[0].astype(BF16)

    def mixers_and_merge(exact_gla):
        def run(za, zb, zg, zs):
            lhs, cg, eg, qe, ca, er, oi = _pre_call(zb, za, zs, cw, up_bd, ub, alv, dtbv, n_lat, ch, exact_gla)
            of, ob = _scan_call(lhs, cg, eg, qe, ca, er, n_lat, ch)
            return _merge_call(oi, of, ob, zg, x, mod3, wa, wb, wo, gla_onorm[0][None, :], gdn_onorm[0][None, :],
                               b_gate[0][None, :], final_g[None, :])
        return run

    alr_norm = jnp.sqrt(jnp.max(nrm[:, :, 0, :N_DIR], axis=(0, 1)))
    up_norm = jnp.sqrt(jnp.max(jnp.sum(gla_up[0] * gla_up[0], axis=1), axis=-1))
    decay_bound = CHUNK * (alr_norm * up_norm + jnp.max(jnp.abs(gla_ub[0]), axis=-1) + jnp.log(2.0)) / GLA_TAU
    return lax.cond(jnp.max(decay_bound) > GLA_FACTOR_LIMIT, mixers_and_merge(True), mixers_and_merge(False),
                    za, zb, zg, zs)
```
